```python
import jax, jax.numpy as jnp
from jax import lax
import numpy as np

D_MODEL = 1024
BATCH = 2
SEQ = 8192
DEPTH = 1

NORM_EPS = 1e-6
D_MIX = 2 * D_MODEL
GLA_HEADS = 4
GLA_DK = D_MODEL // (2 * GLA_HEADS)
GLA_DV = D_MODEL // GLA_HEADS
GLA_KEY = GLA_HEADS * GLA_DK
GLA_VAL = GLA_HEADS * GLA_DV
GLA_GATE_RANK = 16
GLA_TAU = 16.0
GLA_CHUNK = 64
SSD_INNER = D_MIX - GLA_VAL
SSD_HEAD_DIM = 64
SSD_HEADS = SSD_INNER // SSD_HEAD_DIM
SSD_GROUPS = 2
SSD_STATE = 128
SSD_CONV = 4
SSD_CHUNK = 128
SSD_CONV_DIM = SSD_INNER + 2 * SSD_GROUPS * SSD_STATE
PROJ_SIZES = (GLA_KEY, GLA_KEY, GLA_VAL, GLA_VAL, GLA_GATE_RANK,
              SSD_INNER, SSD_CONV_DIM, SSD_HEADS)
D_PROJ = sum(PROJ_SIZES)
PROJ_SPLITS = tuple(sum(PROJ_SIZES[:i + 1]) for i in range(len(PROJ_SIZES) - 1))

kernel_name = "hybrid_gla_ssd_parallel_heads"


def rms_norm(x, w, eps=NORM_EPS):
    xf = x.astype(jnp.float32)
    y = xf * lax.rsqrt(jnp.mean(xf * xf, axis=-1, keepdims=True) + eps)
    return (y * w.astype(jnp.float32)).astype(x.dtype)


def causal_depthwise_conv(u, w, b):
    out = lax.conv_general_dilated(
        u, w[:, None, :], window_strides=(1,), padding=[(w.shape[0] - 1, 0)],
        dimension_numbers=('NWC', 'WIO', 'NWC'), feature_group_count=u.shape[-1])
    return out + b


def gla_chunked(q, k, v, log_a):
    Bsz, T, H, K = q.shape
    V = v.shape[-1]
    C = GLA_CHUNK
    N = T // C
    q, k, log_a = [t.reshape(Bsz, N, C, H, K) for t in (q, k, log_a)]
    v = v.reshape(Bsz, N, C, H, V)
    b = jnp.cumsum(log_a, axis=2)
    b_last = b[:, :, -1:]
    q_dec = q * jnp.exp(b)
    k_dec = k * jnp.exp(-b)
    k_end = k * jnp.exp(b_last - b)
    causal = jnp.tril(jnp.ones((C, C), dtype=bool))
    att = jnp.einsum('bnihk,bnjhk->bnhij', q_dec, k_dec)
    att = jnp.where(causal, att, 0.0)
    o = jnp.einsum('bnhij,bnjhv->bnihv', att, v)
    kv = jnp.einsum('bnjhk,bnjhv->bnhkv', k_end, v)
    decay = jnp.exp(b_last[:, :, 0])

    def step(S, inp):
        kv_n, d_n = inp
        return S * d_n[..., None] + kv_n, S

    S0 = jnp.zeros((Bsz, H, K, V), q.dtype)
    _, S_prev = lax.scan(step, S0, (jnp.moveaxis(kv, 1, 0), jnp.moveaxis(decay, 1, 0)))
    S_prev = jnp.moveaxis(S_prev, 0, 1)
    o = o + jnp.einsum('bnihk,bnhkv->bnihv', q_dec, S_prev)
    return o.reshape(Bsz, T, H, V)


def ssd_chunked(x, dt, A, Bm, Cm):
    Bsz, T, H, P = x.shape
    G, N = Bm.shape[2], Bm.shape[3]
    Hg = H // G
    L = SSD_CHUNK
    nc = T // L
    x = x.reshape(Bsz, nc, L, G, Hg, P)
    dt = dt.reshape(Bsz, nc, L, G, Hg)
    Bm = Bm.reshape(Bsz, nc, L, G, N)
    Cm = Cm.reshape(Bsz, nc, L, G, N)
    a_cum = jnp.cumsum(dt * A.reshape(G, Hg), axis=2)
    causal = jnp.tril(jnp.ones((L, L), dtype=bool))
    seg = a_cum[:, :, :, None] - a_cum[:, :, None]
    decay_ij = jnp.exp(jnp.where(causal[:, :, None, None], seg, -jnp.inf))
    cb = jnp.einsum('bcign,bcjgn->bcijg', Cm, Bm)
    w = cb[..., None] * decay_ij * dt[:, :, None]
    y = jnp.einsum('bcijgh,bcjghp->bcighp', w, x)
    decay_end = jnp.exp(a_cum[:, :, -1:] - a_cum)
    states = jnp.einsum('bcjgn,bcjghp->bcghpn', Bm, (decay_end * dt)[..., None] * x)
    chunk_decay = jnp.exp(a_cum[:, :, -1])

    def step(S, inp):
        st, d = inp
        return S * d[..., None, None] + st, S

    S0 = jnp.zeros((Bsz, G, Hg, P, N), x.dtype)
    _, S_prev = lax.scan(step, S0, (jnp.moveaxis(states, 1, 0), jnp.moveaxis(chunk_decay, 1, 0)))
    S_prev = jnp.moveaxis(S_prev, 0, 1)
    y = y + jnp.einsum('bcign,bcghpn,bcigh->bcighp', Cm, S_prev, jnp.exp(a_cum))
    return y.reshape(Bsz, T, H, P)


def setup_inputs(seed: int = 0) -> dict:
    key = jax.random.key(seed)
    ks = jax.random.split(key, 20)
    f32 = jnp.float32
    nrm = lambda k, shape, s: jax.random.normal(k, shape, f32) * s
    dt0 = jnp.exp(jax.random.uniform(ks[10], (DEPTH, SSD_HEADS), f32,
                                     np.log(1e-3).astype(np.float32), np.log(1e-1).astype(np.float32)))
    return {
        "x": nrm(ks[0], (BATCH, SEQ, D_MODEL), 1.0),
        "c": nrm(ks[1], (BATCH, D_MODEL), 1.0),
        "ada_w": nrm(ks[2], (DEPTH, D_MODEL, 3 * D_MODEL), D_MODEL ** -0.5),
        "ada_b": nrm(ks[3], (DEPTH, 3 * D_MODEL), 0.01),
        "pre_norm_w": 1.0 + nrm(ks[4], (DEPTH, D_MODEL), 0.1),
        "w_in": nrm(ks[5], (DEPTH, D_MODEL, D_PROJ), D_MODEL ** -0.5),
        "gla_gate_w2": nrm(ks[6], (DEPTH, GLA_GATE_RANK, GLA_KEY), GLA_GATE_RANK ** -0.5),
        "gla_gate_b": nrm(ks[7], (DEPTH, GLA_KEY), 0.1),
        "gla_norm_w": 1.0 + nrm(ks[8], (DEPTH, GLA_DV), 0.1),
        "conv_w": nrm(ks[9], (DEPTH, SSD_CONV, SSD_CONV_DIM), SSD_CONV ** -0.5),
        "conv_b": nrm(ks[11], (DEPTH, SSD_CONV_DIM), 0.01),
        "dt_bias": dt0 + jnp.log(-jnp.expm1(-dt0)),
        "a_log": jnp.log(jax.random.uniform(ks[12], (DEPTH, SSD_HEADS), f32, 1.0, 16.0)),
        "d_skip": 1.0 + nrm(ks[13], (DEPTH, SSD_HEADS), 0.1),
        "ssd_norm_w": 1.0 + nrm(ks[14], (DEPTH, SSD_INNER), 0.1),
        "w_out": nrm(ks[15], (DEPTH, D_MIX, D_MODEL), D_MIX ** -0.5),
        "post_norm_w": 1.0 + nrm(ks[16], (DEPTH, D_MODEL), 0.1),
    }


def reference(x, c, ada_w, ada_b, pre_norm_w, w_in, gla_gate_w2, gla_gate_b, gla_norm_w,
              conv_w, conv_b, dt_bias, a_log, d_skip, ssd_norm_w, w_out, post_norm_w):
    f32 = jnp.float32
    Bsz, T, _ = x.shape
    c_act = jax.nn.silu(c)
    for i in range(DEPTH):
        ada = c_act @ ada_w[i] + ada_b[i]
        shift, scale, gate = jnp.split(ada, 3, axis=-1)
        h = rms_norm(x, pre_norm_w[i]) * (1.0 + scale[:, None]) + shift[:, None]
        proj = h @ w_in[i]
        q, k, v, g, gate_lr, z, xbc, dt_raw = jnp.split(proj, PROJ_SPLITS, axis=-1)

        qh = q.reshape(Bsz, T, GLA_HEADS, GLA_DK).astype(f32) * (GLA_DK ** -0.5)
        kh = k.reshape(Bsz, T, GLA_HEADS, GLA_DK).astype(f32)
        vh = v.reshape(Bsz, T, GLA_HEADS, GLA_DV).astype(f32)
        log_a = jax.nn.log_sigmoid((gate_lr @ gla_gate_w2[i] + gla_gate_b[i]).astype(f32)) / GLA_TAU
        o = gla_chunked(qh, kh, vh, log_a.reshape(Bsz, T, GLA_HEADS, GLA_DK))
        o = rms_norm(o, gla_norm_w[i]).astype(x.dtype).reshape(Bsz, T, GLA_VAL)
        o = o * jax.nn.silu(g)

        xbc = jax.nn.silu(causal_depthwise_conv(xbc, conv_w[i], conv_b[i]))
        xs, Bm, Cm = jnp.split(xbc, [SSD_INNER, SSD_INNER + SSD_GROUPS * SSD_STATE], axis=-1)
        dt = jax.nn.softplus(dt_raw.astype(f32) + dt_bias[i].astype(f32))
        A = -jnp.exp(a_log[i].astype(f32))
        xs_h = xs.reshape(Bsz, T, SSD_HEADS, SSD_HEAD_DIM).astype(f32)
        y = ssd_chunked(xs_h, dt, A,
                        Bm.reshape(Bsz, T, SSD_GROUPS, SSD_STATE).astype(f32),
                        Cm.reshape(Bsz, T, SSD_GROUPS, SSD_STATE).astype(f32))
        y = y + d_skip[i].astype(f32)[:, None] * xs_h
        y = y.reshape(Bsz, T, SSD_INNER).astype(x.dtype) * jax.nn.silu(z)
        y = rms_norm(y.reshape(Bsz, T, SSD_GROUPS, SSD_INNER // SSD_GROUPS),
                     ssd_norm_w[i].reshape(SSD_GROUPS, SSD_INNER // SSD_GROUPS))
        y = y.reshape(Bsz, T, SSD_INNER)

        mixed = jnp.concatenate([o, y], axis=-1) @ w_out[i]
        x = x + gate[:, None] * rms_norm(mixed, post_norm_w[i])
    return x
```

```python
import functools

import jax
import jax.numpy as jnp
from jax import lax
from jax.experimental import pallas as pl
from jax.experimental.pallas import tpu as pltpu

F32 = jnp.float32
BF16 = jnp.bfloat16

NORM_EPS = 1e-6
D_MODEL = 1024
GLA_HEADS = 4
GLA_DK = 128
GLA_DV = 256
GLA_KEY = GLA_HEADS * GLA_DK
GLA_VAL = GLA_HEADS * GLA_DV
GLA_GATE_RANK = 16
GLA_TAU = 16.0
GLA_CHUNK = 64
SSD_INNER = 1024
SSD_HEAD_DIM = 64
SSD_HEADS = 16
SSD_GROUPS = 2
SSD_HEADS_PER_GROUP = SSD_HEADS // SSD_GROUPS
SSD_GROUP_WIDTH = SSD_INNER // SSD_GROUPS
SSD_STATE = 128
SSD_CONV = 4
SSD_CHUNK = 128
SSD_CONV_DIM = SSD_INNER + 2 * SSD_GROUPS * SSD_STATE
D_MIX = GLA_VAL + SSD_INNER

LANES = 128
SUBLANES = 8

TIME_BLOCK = 512
COL_QK = 0
COL_V = COL_QK + 2 * GLA_KEY
COL_G = COL_V + GLA_VAL
COL_Z = COL_G + GLA_VAL
COL_XBC = COL_Z + SSD_INNER
W_MAIN_COLS = COL_XBC + SSD_CONV_DIM
DT_LANE = GLA_GATE_RANK
CONV_PAD = SUBLANES
VMEM_LIMIT_BYTES = 56 * 1024 * 1024


def _dot(a, b):
    return jnp.dot(a, b, preferred_element_type=F32)


def _dot_nt(a, b):
    return lax.dot_general(a, b, (((1,), (1,)), ((), ())), preferred_element_type=F32)


def _dot_tn(a, b):
    return lax.dot_general(a, b, (((0,), (0,)), ((), ())), preferred_element_type=F32)


def _split(x):
    hi = x.astype(BF16)
    lo = (x - hi.astype(F32)).astype(BF16)
    return hi, lo


def _dot_split_lhs(a, b_bf16):
    hi, lo = _split(a)
    return _dot(hi, b_bf16) + _dot(lo, b_bf16)


def _dot_split_rhs(a_bf16, b):
    hi, lo = _split(b)
    return _dot(a_bf16, hi) + _dot(a_bf16, lo)


def _silu(x):
    return x * jax.nn.sigmoid(x)


def _log1p_exp_neg_abs(x):
    return jnp.log1p(jnp.exp(-jnp.abs(x)))


def _tril(n, dtype):
    r = lax.broadcasted_iota(jnp.int32, (n, n), 0)
    c = lax.broadcasted_iota(jnp.int32, (n, n), 1)
    return r >= c, (r >= c).astype(dtype)


def _ada_kernel(c_ref, w_ref, b_ref, o_ref):
    c = c_ref[...]
    ca_hi, ca_lo = _split(_silu(c))
    w_hi, w_lo = _split(w_ref[...])
    o_ref[...] = _dot(ca_hi, w_hi) + _dot(ca_hi, w_lo) + _dot(ca_lo, w_hi) + b_ref[...]


def _ada(c_pad, ada_w, ada_b):
    rows = c_pad.shape[0]
    return pl.pallas_call(
        _ada_kernel,
        grid=(3,),
        in_specs=[
            pl.BlockSpec((rows, D_MODEL), lambda j: (0, 0)),
            pl.BlockSpec((D_MODEL, D_MODEL), lambda j: (0, j)),
            pl.BlockSpec((1, D_MODEL), lambda j: (0, j)),
        ],
        out_specs=pl.BlockSpec((rows, D_MODEL), lambda j: (0, j)),
        out_shape=jax.ShapeDtypeStruct((rows, 3 * D_MODEL), F32),
        compiler_params=pltpu.CompilerParams(dimension_semantics=("arbitrary",)),
        name="adaln",
    )(c_pad, ada_w, ada_b)


def _layer_kernel(x_ref, mod_ref, prew_ref, wmain_ref, wsm_ref, w2_ref, gb_ref, gnw_ref,
                  cw_ref, cb_ref, dtb_ref, alog_ref, dskip_ref, snw_ref, expand_ref,
                  wout_ref, postw_ref,
                  out_ref,
                  h_s, qk_s, v_s, sm_s, xbc_s, u_s, la_s, o_s, y_s, mix_s, gstate_s, sstate_s):
    tb = TIME_BLOCK
    t = pl.program_id(1)

    @pl.when(t == 0)
    def _():
        gstate_s[...] = jnp.zeros_like(gstate_s)
        sstate_s[...] = jnp.zeros_like(sstate_s)
        xbc_s[0:CONV_PAD, :] = jnp.zeros((CONV_PAD, SSD_CONV_DIM), F32)

    x = x_ref[0]
    shift = mod_ref[0, 0:1, :]
    scale = mod_ref[0, 1:2, :]
    ms = jnp.mean(x * x, axis=-1, keepdims=True)
    h = (x * lax.rsqrt(ms + NORM_EPS)) * prew_ref[...] * (1.0 + scale) + shift
    h_hi = h.astype(BF16)
    h_lo = (h - h_hi.astype(F32)).astype(BF16)
    h_s[...] = h_hi

    qk_s[...] = _dot(h_hi, wmain_ref[:, COL_QK:COL_V])
    v_s[...] = _dot(h_hi, wmain_ref[:, COL_V:COL_G]).astype(BF16)
    xbc_s[CONV_PAD:CONV_PAD + tb, :] = _dot(h_hi, wmain_ref[:, COL_XBC:W_MAIN_COLS])
    wsm_hi = wsm_ref[0]
    wsm_lo = wsm_ref[1]
    sm = _dot(h_hi, wsm_hi) + _dot(h_hi, wsm_lo) + _dot(h_lo, wsm_hi)
    sm_s[...] = sm

    sm_hi, sm_lo = _split(sm)
    gl = _dot(sm_hi, w2_ref[0]) + _dot(sm_hi, w2_ref[1]) + _dot(sm_lo, w2_ref[0]) + gb_ref[...]
    la_s[...] = (jnp.minimum(gl, 0.0) - _log1p_exp_neg_abs(gl)) * (1.0 / GLA_TAU)

    causal_g, tril_g = _tril(GLA_CHUNK, BF16)

    def gla_chunk(ci, carry):
        r = pl.multiple_of(ci * GLA_CHUNK, GLA_CHUNK)
        rows = pl.ds(r, GLA_CHUNK)
        b = _dot_split_rhs(tril_g, la_s[rows, :])
        b_last = b[GLA_CHUNK - 1:GLA_CHUNK, :]
        q = qk_s[rows, 0:GLA_KEY]
        k = qk_s[rows, GLA_KEY:2 * GLA_KEY]
        q_dec = (q * (GLA_DK ** -0.5) * jnp.exp(b)).astype(BF16)
        k_dec = (k * jnp.exp(-b)).astype(BF16)
        k_end = (k * jnp.exp(b_last - b)).astype(BF16)
        decay = jnp.exp(b_last)
        for hh in range(GLA_HEADS):
            ks = slice(hh * GLA_DK, (hh + 1) * GLA_DK)
            vs = slice(hh * GLA_DV, (hh + 1) * GLA_DV)
            att = _dot_nt(q_dec[:, ks], k_dec[:, ks])
            att = jnp.where(causal_g, att, 0.0).astype(BF16)
            v_h = v_s[rows, vs]
            state_t = gstate_s[hh]
            o = _dot(att, v_h) + _dot_nt(q_dec[:, ks], state_t.astype(BF16))
            o_s[rows, vs] = o
            gstate_s[hh] = state_t * decay[:, ks] + _dot_tn(v_h, k_end[:, ks])
        return carry

    lax.fori_loop(0, tb // GLA_CHUNK, gla_chunk, 0)

    g_act = _silu(_dot(h_hi, wmain_ref[:, COL_G:COL_Z]))
    for hh in range(GLA_HEADS):
        vs = slice(hh * GLA_DV, (hh + 1) * GLA_DV)
        o = o_s[:, vs]
        o = (o * lax.rsqrt(jnp.mean(o * o, axis=-1, keepdims=True) + NORM_EPS)) * gnw_ref[...]
        mix_s[:, vs] = (o * g_act[:, vs]).astype(BF16)

    causal_s, tril_s = _tril(SSD_CHUNK, BF16)
    a_neg = -jnp.exp(alog_ref[...])
    expand = expand_ref[...]

    for rc in range(tb // SSD_CHUNK):
        base = rc * SSD_CHUNK + CONV_PAD - (SSD_CONV - 1)
        acc = cb_ref[...] + cw_ref[0:1, :] * xbc_s[base:base + SSD_CHUNK, :]
        for kk in range(1, SSD_CONV):
            acc = acc + cw_ref[kk:kk + 1, :] * xbc_s[base + kk:base + kk + SSD_CHUNK, :]
        u_s[rc * SSD_CHUNK:(rc + 1) * SSD_CHUNK, :] = _silu(acc)

    def ssd_chunk(ci, carry):
        r = pl.multiple_of(ci * SSD_CHUNK, SSD_CHUNK)
        rows = pl.ds(r, SSD_CHUNK)
        xs = u_s[rows, 0:SSD_INNER]
        bm = u_s[rows, SSD_INNER:SSD_INNER + SSD_GROUPS * SSD_STATE].astype(BF16)
        cm = u_s[rows, SSD_INNER + SSD_GROUPS * SSD_STATE:SSD_CONV_DIM].astype(BF16)

        dt_in = sm_s[rows, :] + dtb_ref[...]
        dt = jnp.maximum(dt_in, 0.0) + _log1p_exp_neg_abs(dt_in)
        a_cum = _dot_split_rhs(tril_s, dt * a_neg)
        a_cum_t = a_cum.T
        dt_t = dt.T
        a_last = a_cum[SSD_CHUNK - 1:SSD_CHUNK, :]
        decay_in = jnp.exp(a_cum)
        decay_end_dt = jnp.exp(a_last - a_cum) * dt
        decay_in_x = _dot_split_lhs(decay_in, expand)
        decay_end_dt_x = _dot_split_lhs(decay_end_dt, expand)
        chunk_decay_x = decay_in_x[SSD_CHUNK - 1:SSD_CHUNK, :]
        xs_b = xs.astype(BF16)
        xs_scaled = (xs * decay_end_dt_x).astype(BF16)

        for g in range(SSD_GROUPS):
            gs = slice(g * SSD_GROUP_WIDTH, (g + 1) * SSD_GROUP_WIDTH)
            ns = slice(g * SSD_STATE, (g + 1) * SSD_STATE)
            b_g = bm[:, ns]
            c_g = cm[:, ns]
            cb = _dot_nt(c_g, b_g)
            state = sstate_s[g]
            y_g = _dot(c_g, state.astype(BF16)) * decay_in_x[:, gs]
            sstate_s[g] = state * chunk_decay_x[:, gs] + _dot_tn(b_g, xs_scaled[:, gs])
            y_heads = []
            for hl in range(SSD_HEADS_PER_GROUP):
                lane = DT_LANE + g * SSD_HEADS_PER_GROUP + hl
                seg = a_cum[:, lane:lane + 1] - a_cum_t[lane:lane + 1, :]
                w = jnp.where(causal_s, jnp.exp(seg), 0.0) * cb * dt_t[lane:lane + 1, :]
                ps = slice(g * SSD_GROUP_WIDTH + hl * SSD_HEAD_DIM,
                           g * SSD_GROUP_WIDTH + (hl + 1) * SSD_HEAD_DIM)
                y_heads.append(_dot(w.astype(BF16), xs_b[:, ps]))
            y_g = y_g + jnp.concatenate(y_heads, axis=1)
            y_s[rows, gs] = y_g + dskip_ref[:, gs] * xs[:, gs]
        return carry

    lax.fori_loop(0, tb // SSD_CHUNK, ssd_chunk, 0)

    xbc_s[0:CONV_PAD, :] = xbc_s[tb:tb + CONV_PAD, :]

    z_act = _silu(_dot(h_hi, wmain_ref[:, COL_Z:COL_XBC]))
    for g in range(SSD_GROUPS):
        gs = slice(g * SSD_GROUP_WIDTH, (g + 1) * SSD_GROUP_WIDTH)
        y = y_s[:, gs] * z_act[:, gs]
        y = (y * lax.rsqrt(jnp.mean(y * y, axis=-1, keepdims=True) + NORM_EPS)) * snw_ref[:, gs]
        mix_s[:, GLA_VAL + g * SSD_GROUP_WIDTH:GLA_VAL + (g + 1) * SSD_GROUP_WIDTH] = y.astype(BF16)

    mixed = _dot(mix_s[...], wout_ref[...])
    mixed = (mixed * lax.rsqrt(jnp.mean(mixed * mixed, axis=-1, keepdims=True) + NORM_EPS)) * postw_ref[...]
    gate = mod_ref[0, 2:3, :]
    out_ref[0] = x_ref[0] + gate * mixed


def _const_spec(shape):
    zeros = (0,) * len(shape)
    return pl.BlockSpec(shape, lambda b, t: zeros, pipeline_mode=pl.Buffered(1))


def _layer(x, mod, prew, wmain, wsm, w2, gb, gnw, cw, cb, dtb, alog, dskip, snw, expand, wout, postw):
    bsz, seq, _ = x.shape
    tb = TIME_BLOCK
    consts = (prew, wmain, wsm, w2, gb, gnw, cw, cb, dtb, alog, dskip, snw, expand, wout, postw)
    return pl.pallas_call(
        _layer_kernel,
        grid=(bsz, seq // tb),
        in_specs=[
            pl.BlockSpec((1, tb, D_MODEL), lambda b, t: (b, t, 0)),
            pl.BlockSpec((1, 3, D_MODEL), lambda b, t: (b, 0, 0)),
        ] + [_const_spec(a.shape) for a in consts],
        out_specs=pl.BlockSpec((1, tb, D_MODEL), lambda b, t: (b, t, 0)),
        out_shape=jax.ShapeDtypeStruct(x.shape, x.dtype),
        scratch_shapes=[
            pltpu.VMEM((tb, D_MODEL), BF16),
            pltpu.VMEM((tb, 2 * GLA_KEY), F32),
            pltpu.VMEM((tb, GLA_VAL), BF16),
            pltpu.VMEM((tb, LANES), F32),
            pltpu.VMEM((tb + CONV_PAD, SSD_CONV_DIM), F32),
            pltpu.VMEM((tb, SSD_CONV_DIM), F32),
            pltpu.VMEM((tb, GLA_KEY), F32),
            pltpu.VMEM((tb, GLA_VAL), F32),
            pltpu.VMEM((tb, SSD_INNER), F32),
            pltpu.VMEM((tb, D_MIX), BF16),
            pltpu.VMEM((GLA_HEADS, GLA_DV, GLA_DK), F32),
            pltpu.VMEM((SSD_GROUPS, SSD_STATE, SSD_GROUP_WIDTH), F32),
        ],
        compiler_params=pltpu.CompilerParams(
            dimension_semantics=("arbitrary", "arbitrary"),
            vmem_limit_bytes=VMEM_LIMIT_BYTES),
        name="hybrid_layer",
    )(x, mod, *consts)


def _lane_pad(v, offset):
    return jnp.zeros((1, LANES), F32).at[0, offset:offset + v.shape[0]].set(v.astype(F32))


def kernel(x, c, ada_w, ada_b, pre_norm_w, w_in, gla_gate_w2, gla_gate_b, gla_norm_w, conv_w, conv_b,
           dt_bias, a_log, d_skip, ssd_norm_w, w_out, post_norm_w):
    bsz, seq, _ = x.shape
    assert seq % TIME_BLOCK == 0
    depth = ada_w.shape[0]
    head_of_channel = jnp.arange(SSD_INNER, dtype=jnp.int32) // SSD_HEAD_DIM + DT_LANE
    expand = (lax.broadcasted_iota(jnp.int32, (LANES, SSD_INNER), 0) == head_of_channel[None, :]).astype(BF16)
    c_pad = jnp.zeros((SUBLANES, D_MODEL), F32).at[:bsz].set(c)
    for i in range(depth):
        ada = _ada(c_pad, ada_w[i], ada_b[i][None, :])
        mod = ada[:bsz].reshape(bsz, 3, D_MODEL)

        wi = w_in[i]
        o_gl = 2 * GLA_KEY + 2 * GLA_VAL
        o_z = o_gl + GLA_GATE_RANK
        o_xbc = o_z + SSD_INNER
        o_dt = o_xbc + SSD_CONV_DIM
        wmain = jnp.concatenate([wi[:, :o_gl], wi[:, o_z:o_dt]], axis=1).astype(BF16)
        wsm = jnp.zeros((D_MODEL, LANES), F32)
        wsm = wsm.at[:, 0:GLA_GATE_RANK].set(wi[:, o_gl:o_z])
        wsm = wsm.at[:, DT_LANE:DT_LANE + SSD_HEADS].set(wi[:, o_dt:])
        wsm_hi = wsm.astype(BF16)
        wsm_lo = (wsm - wsm_hi.astype(F32)).astype(BF16)
        w2 = jnp.zeros((LANES, GLA_KEY), F32).at[0:GLA_GATE_RANK].set(gla_gate_w2[i])
        w2_hi = w2.astype(BF16)
        w2_lo = (w2 - w2_hi.astype(F32)).astype(BF16)

        x = _layer(
            x, mod, pre_norm_w[i][None, :], wmain,
            jnp.stack([wsm_hi, wsm_lo]), jnp.stack([w2_hi, w2_lo]),
            gla_gate_b[i][None, :], gla_norm_w[i][None, :],
            conv_w[i], conv_b[i][None, :],
            _lane_pad(dt_bias[i], DT_LANE), _lane_pad(a_log[i], DT_LANE),
            jnp.repeat(d_skip[i].astype(F32), SSD_HEAD_DIM)[None, :],
            ssd_norm_w[i][None, :], expand,
            w_out[i].astype(BF16), post_norm_w[i][None, :])
    return x
```

```python
import jax
import jax.numpy as jnp
from jax import lax
from jax.experimental import pallas as pl
from jax.experimental.pallas import tpu as pltpu

F32 = jnp.float32
BF16 = jnp.bfloat16

NORM_EPS = 1e-6
D_MODEL = 1024
GLA_HEADS = 4
GLA_DK = 128
GLA_DV = 256
GLA_KEY = GLA_HEADS * GLA_DK
GLA_VAL = GLA_HEADS * GLA_DV
GLA_GATE_RANK = 16
GLA_TAU = 16.0
GLA_CHUNK = 64
SSD_INNER = 1024
SSD_HEAD_DIM = 64
SSD_HEADS = 16
SSD_GROUPS = 2
SSD_HEADS_PER_GROUP = SSD_HEADS // SSD_GROUPS
SSD_GROUP_WIDTH = SSD_INNER // SSD_GROUPS
SSD_STATE = 128
SSD_CONV = 4
SSD_CHUNK = 128
SSD_CONV_DIM = SSD_INNER + 2 * SSD_GROUPS * SSD_STATE
D_MIX = GLA_VAL + SSD_INNER

LANES = 128
SUBLANES = 8

TIME_BLOCK = 512
ROW_CHUNK = 128
PROJ_TILE = 512
OUT_SPLITS = 2
COL_QK = 0
COL_V = COL_QK + 2 * GLA_KEY
COL_G = COL_V + GLA_VAL
COL_Z = COL_G + GLA_VAL
COL_XBC = COL_Z + SSD_INNER
W_MAIN_COLS = COL_XBC + SSD_CONV_DIM
DT_LANE = GLA_GATE_RANK
CONV_PAD = SUBLANES
VMEM_LIMIT_BYTES = 56 * 1024 * 1024


def _dot(a, b):
    return jnp.dot(a, b, preferred_element_type=F32)


def _dot_nt(a, b):
    return lax.dot_general(a, b, (((1,), (1,)), ((), ())), preferred_element_type=F32)


def _dot_tn(a, b):
    return lax.dot_general(a, b, (((0,), (0,)), ((), ())), preferred_element_type=F32)


def _split(x):
    hi = x.astype(BF16)
    lo = (x - hi.astype(F32)).astype(BF16)
    return hi, lo


def _silu(x):
    return x * jax.nn.sigmoid(x)


def _log1p_exp_neg_abs(x):
    return jnp.log1p(jnp.exp(-jnp.abs(x)))


def _tril(n, dtype):
    r = lax.broadcasted_iota(jnp.int32, (n, n), 0)
    c = lax.broadcasted_iota(jnp.int32, (n, n), 1)
    return r >= c, (r >= c).astype(dtype)


def _ada_kernel(c_ref, w_ref, b_ref, o_ref):
    c = c_ref[...]
    ca_hi, ca_lo = _split(_silu(c))
    w_hi, w_lo = _split(w_ref[...])
    o_ref[...] = _dot(ca_hi, w_hi) + _dot(ca_hi, w_lo) + _dot(ca_lo, w_hi) + b_ref[...]


def _ada(c_pad, ada_w, ada_b):
    rows = c_pad.shape[0]
    return pl.pallas_call(
        _ada_kernel,
        grid=(3,),
        in_specs=[
            pl.BlockSpec((rows, D_MODEL), lambda j: (0, 0)),
            pl.BlockSpec((D_MODEL, D_MODEL), lambda j: (0, j)),
            pl.BlockSpec((1, D_MODEL), lambda j: (0, j)),
        ],
        out_specs=pl.BlockSpec((rows, D_MODEL), lambda j: (0, j)),
        out_shape=jax.ShapeDtypeStruct((rows, 3 * D_MODEL), F32),
        compiler_params=pltpu.CompilerParams(dimension_semantics=("arbitrary",)),
        name="adaln",
    )(c_pad, ada_w, ada_b)


def _layer_kernel(x_ref, mod_ref, prew_ref, wmain_ref, wsm_ref, w2_ref, gb_ref, gnw_ref,
                  cw_ref, cb_ref, dtb_ref, alog_ref, dskip_ref, snw_ref, expand_ref,
                  wout_ref, postw_ref,
                  out_ref,
                  h_s, sm_s, xbc_s, u_s, b_s, qd_s, kd_s, ke_s, dec_s, v_s, gact_s, zact_s, mix_s,
                  gstate_s, sstate_s):
    tb = TIME_BLOCK
    t = pl.program_id(1)

    @pl.when(t == 0)
    def _():
        gstate_s[...] = jnp.zeros_like(gstate_s)
        sstate_s[...] = jnp.zeros_like(sstate_s)
        xbc_s[0:CONV_PAD, :] = jnp.zeros((CONV_PAD, SSD_CONV_DIM), F32)

    shift = mod_ref[0, 0:1, :]
    scale1 = 1.0 + mod_ref[0, 1:2, :]
    gate = mod_ref[0, 2:3, :]
    row_chunks = [slice(r, r + ROW_CHUNK) for r in range(0, tb, ROW_CHUNK)]

    for rs in row_chunks:
        x = x_ref[0, rs, :]
        ms = jnp.mean(x * x, axis=-1, keepdims=True)
        h = (x * lax.rsqrt(ms + NORM_EPS)) * prew_ref[...] * scale1 + shift
        h_hi = h.astype(BF16)
        h_lo = (h - h_hi.astype(F32)).astype(BF16)
        h_s[rs, :] = h_hi
        both = _dot(h_hi, wsm_ref[...])
        sm_s[rs, :] = both[:, 0:LANES] + both[:, LANES:2 * LANES] + _dot(h_lo, wsm_ref[:, 0:LANES])

    def proj(col, width):
        return _dot(h_s[...], wmain_ref[:, col:col + width])

    xbc_rows = slice(CONV_PAD, CONV_PAD + tb)
    xbc_s[xbc_rows, 0:PROJ_TILE] = proj(COL_XBC, PROJ_TILE)

    causal_g, tril_g = _tril(GLA_CHUNK, BF16)
    tril_g2 = jnp.concatenate([tril_g, tril_g], axis=1)
    for rs in row_chunks:
        sm_hi, sm_lo = _split(sm_s[rs, :])
        sm_both = jnp.concatenate([sm_hi, sm_lo], axis=1)
        gl = _dot(sm_both, w2_ref[0]) + _dot(sm_both, w2_ref[1]) + gb_ref[...]
        la = (jnp.minimum(gl, 0.0) - _log1p_exp_neg_abs(gl)) * (1.0 / GLA_TAU)
        for cc in range(ROW_CHUNK // GLA_CHUNK):
            la_hi, la_lo = _split(la[cc * GLA_CHUNK:(cc + 1) * GLA_CHUNK, :])
            b = _dot(tril_g2, jnp.concatenate([la_hi, la_lo], axis=0))
            r0 = rs.start + cc * GLA_CHUNK
            b_s[r0:r0 + GLA_CHUNK, :] = b
            ci = r0 // GLA_CHUNK
            dec_s[ci:ci + 1, :] = jnp.exp(b[GLA_CHUNK - 1:GLA_CHUNK, :])

    def conv_tile(col):
        cs = slice(col, col + PROJ_TILE)
        for rs in row_chunks:
            base = rs.start + CONV_PAD - (SSD_CONV - 1)
            acc = cb_ref[:, cs] + cw_ref[0:1, cs] * xbc_s[base:base + ROW_CHUNK, cs]
            for kk in range(1, SSD_CONV):
                acc = acc + cw_ref[kk:kk + 1, cs] * xbc_s[base + kk:base + kk + ROW_CHUNK, cs]
            u_s[rs, cs] = _silu(acc)

    xbc_s[xbc_rows, PROJ_TILE:2 * PROJ_TILE] = proj(COL_XBC + PROJ_TILE, PROJ_TILE)
    conv_tile(0)
    xbc_s[xbc_rows, 2 * PROJ_TILE:3 * PROJ_TILE] = proj(COL_XBC + 2 * PROJ_TILE, PROJ_TILE)
    conv_tile(PROJ_TILE)

    q = proj(COL_QK, GLA_KEY)
    qd_s[...] = (q * (GLA_DK ** -0.5) * jnp.exp(b_s[...])).astype(BF16)
    conv_tile(2 * PROJ_TILE)

    k = proj(COL_QK + GLA_KEY, GLA_KEY)
    for ci in range(tb // GLA_CHUNK):
        rows = slice(ci * GLA_CHUNK, (ci + 1) * GLA_CHUNK)
        b = b_s[rows, :]
        k_c = k[rows, :]
        kd_s[rows, :] = (k_c * jnp.exp(-b)).astype(BF16)
        ke_s[rows, :] = (k_c * jnp.exp(b[GLA_CHUNK - 1:GLA_CHUNK, :] - b)).astype(BF16)

    for half in range(GLA_VAL // PROJ_TILE):
        cs = slice(half * PROJ_TILE, (half + 1) * PROJ_TILE)
        gact_s[:, cs] = _silu(proj(COL_G + half * PROJ_TILE, PROJ_TILE)).astype(BF16)
    for half in range(SSD_INNER // PROJ_TILE):
        cs = slice(half * PROJ_TILE, (half + 1) * PROJ_TILE)
        zact_s[:, cs] = _silu(proj(COL_Z + half * PROJ_TILE, PROJ_TILE)).astype(BF16)
    for half in range(GLA_VAL // PROJ_TILE):
        cs = slice(half * PROJ_TILE, (half + 1) * PROJ_TILE)
        v_s[:, cs] = proj(COL_V + half * PROJ_TILE, PROJ_TILE).astype(BF16)

    def gla_chunk(ci):
        rows = slice(ci * GLA_CHUNK, (ci + 1) * GLA_CHUNK)
        decay = dec_s[ci:ci + 1, :]
        for hh in range(GLA_HEADS):
            ks = slice(hh * GLA_DK, (hh + 1) * GLA_DK)
            vs = slice(hh * GLA_DV, (hh + 1) * GLA_DV)
            q_dec = qd_s[rows, ks]
            att = _dot_nt(q_dec, kd_s[rows, ks])
            att = jnp.where(causal_g, att, 0.0).astype(BF16)
            v_h = v_s[rows, vs]
            state_t = gstate_s[hh]
            o = _dot(att, v_h) + _dot_nt(q_dec, state_t.astype(BF16))
            gstate_s[hh] = state_t * decay[:, ks] + _dot_tn(v_h, ke_s[rows, ks])
            o = (o * lax.rsqrt(jnp.mean(o * o, axis=-1, keepdims=True) + NORM_EPS)) * gnw_ref[...]
            mix_s[rows, vs] = (o * gact_s[rows, vs].astype(F32)).astype(BF16)

    causal_s, tril_s = _tril(SSD_CHUNK, BF16)
    tril_s2 = jnp.concatenate([tril_s, tril_s], axis=1)
    a_neg = -jnp.exp(alog_ref[...])

    def expand_heads(v):
        hi, lo = _split(v)
        return _dot(jnp.concatenate([hi, lo], axis=1), expand_ref[...])

    def ssd_chunk(ci):
        rows = slice(ci * SSD_CHUNK, (ci + 1) * SSD_CHUNK)
        xs = u_s[rows, 0:SSD_INNER]
        bm = u_s[rows, SSD_INNER:SSD_INNER + SSD_GROUPS * SSD_STATE].astype(BF16)
        cm = u_s[rows, SSD_INNER + SSD_GROUPS * SSD_STATE:SSD_CONV_DIM].astype(BF16)

        dt_in = sm_s[rows, :] + dtb_ref[...]
        dt = jnp.maximum(dt_in, 0.0) + _log1p_exp_neg_abs(dt_in)
        dta_hi, dta_lo = _split(dt * a_neg)
        a_cum = _dot(tril_s2, jnp.concatenate([dta_hi, dta_lo], axis=0))
        a_cum_t = a_cum.T
        dt_t = dt.T
        a_last = a_cum[SSD_CHUNK - 1:SSD_CHUNK, :]
        decay_in_x = expand_heads(jnp.exp(a_cum))
        decay_end_dt_x = expand_heads(jnp.exp(a_last - a_cum) * dt)
        chunk_decay_x = decay_in_x[SSD_CHUNK - 1:SSD_CHUNK, :]
        xs_b = xs.astype(BF16)
        xs_scaled = (xs * decay_end_dt_x).astype(BF16)

        for g in range(SSD_GROUPS):
            gs = slice(g * SSD_GROUP_WIDTH, (g + 1) * SSD_GROUP_WIDTH)
            ns = slice(g * SSD_STATE, (g + 1) * SSD_STATE)
            b_g = bm[:, ns]
            c_g = cm[:, ns]
            cb = _dot_nt(c_g, b_g)
            state = sstate_s[g]
            y_g = _dot(c_g, state.astype(BF16)) * decay_in_x[:, gs]
            sstate_s[g] = state * chunk_decay_x[:, gs] + _dot_tn(b_g, xs_scaled[:, gs])
            y_heads = []
            for hl in range(SSD_HEADS_PER_GROUP):
                lane = DT_LANE + g * SSD_HEADS_PER_GROUP + hl
                seg = a_cum[:, lane:lane + 1] - a_cum_t[lane:lane + 1, :]
                w = jnp.where(causal_s, jnp.exp(seg), 0.0) * cb * dt_t[lane:lane + 1, :]
                ps = slice(g * SSD_GROUP_WIDTH + hl * SSD_HEAD_DIM,
                           g * SSD_GROUP_WIDTH + (hl + 1) * SSD_HEAD_DIM)
                y_heads.append(_dot(w.astype(BF16), xs_b[:, ps]))
            y_g = y_g + jnp.concatenate(y_heads, axis=1) + dskip_ref[:, gs] * xs[:, gs]
            y_g = y_g * zact_s[rows, gs].astype(F32)
            y_g = (y_g * lax.rsqrt(jnp.mean(y_g * y_g, axis=-1, keepdims=True) + NORM_EPS)) * snw_ref[:, gs]
            mix_s[rows, GLA_VAL + g * SSD_GROUP_WIDTH:GLA_VAL + (g + 1) * SSD_GROUP_WIDTH] = y_g.astype(BF16)

    def out_rows(rs):
        mixed = _dot(mix_s[rs, :], wout_ref[...])
        mixed = (mixed * lax.rsqrt(jnp.mean(mixed * mixed, axis=-1, keepdims=True) + NORM_EPS)) * postw_ref[...]
        out_ref[0, rs, :] = x_ref[0, rs, :] + gate * mixed

    gla_per_ssd = SSD_CHUNK // GLA_CHUNK
    n_ssd = tb // SSD_CHUNK
    for ci in range(n_ssd):
        for cc in range(gla_per_ssd):
            gla_chunk(ci * gla_per_ssd + cc)
        ssd_chunk(ci)
        if (ci + 1) % (n_ssd // OUT_SPLITS) == 0:
            r1 = (ci + 1) * SSD_CHUNK
            out_rows(slice(r1 - tb // OUT_SPLITS, r1))

    xbc_s[0:CONV_PAD, :] = xbc_s[tb:tb + CONV_PAD, :]


def _const_spec(shape):
    zeros = (0,) * len(shape)
    return pl.BlockSpec(shape, lambda b, t: zeros, pipeline_mode=pl.Buffered(1))


def _layer(x, mod, prew, wmain, wsm, w2, gb, gnw, cw, cb, dtb, alog, dskip, snw, expand, wout, postw):
    bsz, seq, _ = x.shape
    tb = TIME_BLOCK
    consts = (prew, wmain, wsm, w2, gb, gnw, cw, cb, dtb, alog, dskip, snw, expand, wout, postw)
    return pl.pallas_call(
        _layer_kernel,
        grid=(bsz, seq // tb),
        in_specs=[
            pl.BlockSpec((1, tb, D_MODEL), lambda b, t: (b, t, 0)),
            pl.BlockSpec((1, 3, D_MODEL), lambda b, t: (b, 0, 0)),
        ] + [_const_spec(a.shape) for a in consts],
        out_specs=pl.BlockSpec((1, tb, D_MODEL), lambda b, t: (b, t, 0)),
        out_shape=jax.ShapeDtypeStruct(x.shape, x.dtype),
        scratch_shapes=[
            pltpu.VMEM((tb, D_MODEL), BF16),
            pltpu.VMEM((tb, LANES), F32),
            pltpu.VMEM((tb + CONV_PAD, SSD_CONV_DIM), F32),
            pltpu.VMEM((tb, SSD_CONV_DIM), F32),
            pltpu.VMEM((tb, GLA_KEY), F32),
            pltpu.VMEM((tb, GLA_KEY), BF16),
            pltpu.VMEM((tb, GLA_KEY), BF16),
            pltpu.VMEM((tb, GLA_KEY), BF16),
            pltpu.VMEM((tb // GLA_CHUNK, GLA_KEY), F32),
            pltpu.VMEM((tb, GLA_VAL), BF16),
            pltpu.VMEM((tb, GLA_VAL), BF16),
            pltpu.VMEM((tb, SSD_INNER), BF16),
            pltpu.VMEM((tb, D_MIX), BF16),
            pltpu.VMEM((GLA_HEADS, GLA_DV, GLA_DK), F32),
            pltpu.VMEM((SSD_GROUPS, SSD_STATE, SSD_GROUP_WIDTH), F32),
        ],
        compiler_params=pltpu.CompilerParams(
            dimension_semantics=("arbitrary", "arbitrary"),
            vmem_limit_bytes=VMEM_LIMIT_BYTES),
        name="hybrid_layer",
    )(x, mod, *consts)


def _lane_pad(v, offset):
    return jnp.zeros((1, LANES), F32).at[0, offset:offset + v.shape[0]].set(v.astype(F32))


def _hi_lo(w):
    hi = w.astype(BF16)
    return hi, (w - hi.astype(F32)).astype(BF16)


def kernel(x, c, ada_w, ada_b, pre_norm_w, w_in, gla_gate_w2, gla_gate_b, gla_norm_w, conv_w, conv_b,
           dt_bias, a_log, d_skip, ssd_norm_w, w_out, post_norm_w):
    bsz, seq, _ = x.shape
    assert seq % TIME_BLOCK == 0
    depth = ada_w.shape[0]
    head_of_channel = jnp.arange(SSD_INNER, dtype=jnp.int32) // SSD_HEAD_DIM + DT_LANE
    expand = (lax.broadcasted_iota(jnp.int32, (LANES, SSD_INNER), 0) == head_of_channel[None, :]).astype(BF16)
    expand2 = jnp.concatenate([expand, expand], axis=0)
    c_pad = jnp.zeros((SUBLANES, D_MODEL), F32).at[:bsz].set(c)
    for i in range(depth):
        ada = _ada(c_pad, ada_w[i], ada_b[i][None, :])
        mod = ada[:bsz].reshape(bsz, 3, D_MODEL)

        wi = w_in[i]
        o_gl = 2 * GLA_KEY + 2 * GLA_VAL
        o_z = o_gl + GLA_GATE_RANK
        o_xbc = o_z + SSD_INNER
        o_dt = o_xbc + SSD_CONV_DIM
        wmain = jnp.concatenate([wi[:, :o_gl], wi[:, o_z:o_dt]], axis=1).astype(BF16)
        wsm = jnp.zeros((D_MODEL, LANES), F32)
        wsm = wsm.at[:, 0:GLA_GATE_RANK].set(wi[:, o_gl:o_z])
        wsm = wsm.at[:, DT_LANE:DT_LANE + SSD_HEADS].set(wi[:, o_dt:])
        wsm2 = jnp.concatenate(_hi_lo(wsm), axis=1)
        w2 = jnp.zeros((LANES, GLA_KEY), F32).at[0:GLA_GATE_RANK].set(gla_gate_w2[i])
        w2_hi, w2_lo = _hi_lo(w2)
        w2_planes = jnp.stack([jnp.concatenate([w2_hi, w2_hi], axis=0),
                               jnp.concatenate([w2_lo, jnp.zeros_like(w2_lo)], axis=0)])

        x = _layer(
            x, mod, pre_norm_w[i][None, :], wmain, wsm2, w2_planes,
            gla_gate_b[i][None, :], gla_norm_w[i][None, :],
            conv_w[i], conv_b[i][None, :],
            _lane_pad(dt_bias[i], DT_LANE), _lane_pad(a_log[i], DT_LANE),
            jnp.repeat(d_skip[i].astype(F32), SSD_HEAD_DIM)[None, :],
            ssd_norm_w[i][None, :], expand2,
            w_out[i].astype(BF16), post_norm_w[i][None, :])
    return x
```

```python
import jax
import jax.numpy as jnp
from jax import lax
from jax.experimental import pallas as pl
from jax.experimental.pallas import tpu as pltpu

F32 = jnp.float32
BF16 = jnp.bfloat16

NORM_EPS = 1e-6
D_MODEL = 1024
GLA_HEADS = 4
GLA_DK = 128
GLA_DV = 256
GLA_KEY = GLA_HEADS * GLA_DK
GLA_VAL = GLA_HEADS * GLA_DV
GLA_GATE_RANK = 16
GLA_TAU = 16.0
GLA_CHUNK = 64
SSD_INNER = 1024
SSD_HEAD_DIM = 64
SSD_HEADS = 16
SSD_GROUPS = 2
SSD_HEADS_PER_GROUP = SSD_HEADS // SSD_GROUPS
SSD_GROUP_WIDTH = SSD_INNER // SSD_GROUPS
SSD_STATE = 128
SSD_CONV = 4
SSD_CHUNK = 128
SSD_CONV_DIM = SSD_INNER + 2 * SSD_GROUPS * SSD_STATE
D_MIX = GLA_VAL + SSD_INNER

LANES = 128
SUBLANES = 8

TIME_BLOCK = 512
ROW_CHUNK = 128
PROJ_TILE = 512
SUB_BLOCK = 256
COL_QK = 0
COL_V = COL_QK + 2 * GLA_KEY
COL_G = COL_V + GLA_VAL
COL_Z = COL_G + GLA_VAL
COL_XBC = COL_Z + SSD_INNER
W_MAIN_COLS = COL_XBC + SSD_CONV_DIM
DT_LANE = GLA_GATE_RANK
assert DT_LANE + SSD_HEADS <= LANES
IN_GATE = 2 * GLA_KEY + 2 * GLA_VAL
IN_Z = IN_GATE + GLA_GATE_RANK
IN_DT = IN_Z + SSD_INNER + SSD_CONV_DIM
D_PROJ = IN_DT + SSD_HEADS
assert IN_GATE % LANES == 0 and IN_DT % LANES == DT_LANE
PREP_ROWS = 128
(VEC_PRE_W, VEC_GATE_B, VEC_GLA_NORM, VEC_CONV_B, VEC_CONV_W, VEC_DT_BIAS, VEC_A_LOG,
 VEC_D_SKIP, VEC_SSD_NORM, VEC_POST_W) = (0, 1, 2, 3, 4, 4 + SSD_CONV, 5 + SSD_CONV, 6 + SSD_CONV,
                                          7 + SSD_CONV, 8 + SSD_CONV)
VEC_USED = VEC_POST_W + 1
VEC_ROWS = -(-VEC_USED // SUBLANES) * SUBLANES
VEC_COLS = SSD_CONV_DIM
CONV_PAD = SUBLANES
VMEM_LIMIT_BYTES = 56 * 1024 * 1024


def _dot(a, b):
    return jnp.dot(a, b, preferred_element_type=F32)


def _dot_nt(a, b):
    return lax.dot_general(a, b, (((1,), (1,)), ((), ())), preferred_element_type=F32)


def _dot_tn(a, b):
    return lax.dot_general(a, b, (((0,), (0,)), ((), ())), preferred_element_type=F32)


def _split(x):
    hi = x.astype(BF16)
    lo = (x - hi.astype(F32)).astype(BF16)
    return hi, lo


def _silu(x):
    hx = 0.5 * x
    return hx + hx * jnp.tanh(hx)


def _log1p_exp_neg_abs(x):
    return jnp.log(1.0 + jnp.exp(-jnp.abs(x)))


def _tril(n, dtype):
    r = lax.broadcasted_iota(jnp.int32, (n, n), 0)
    c = lax.broadcasted_iota(jnp.int32, (n, n), 1)
    return r >= c, (r >= c).astype(dtype)


def _ada_kernel(c_ref, w_ref, b_ref, o_ref):
    c = c_ref[...]
    ca_hi, ca_lo = _split(_silu(c))
    w_hi, w_lo = _split(w_ref[...])
    o_ref[...] = _dot(ca_hi, w_hi) + _dot(ca_hi, w_lo) + _dot(ca_lo, w_hi) + b_ref[...]


def _ada(c_pad, ada_w, ada_b):
    rows = c_pad.shape[0]
    return pl.pallas_call(
        _ada_kernel,
        grid=(3,),
        in_specs=[
            pl.BlockSpec((rows, D_MODEL), lambda j: (0, 0)),
            pl.BlockSpec((D_MODEL, D_MODEL), lambda j: (0, j)),
            pl.BlockSpec((1, D_MODEL), lambda j: (0, j)),
        ],
        out_specs=pl.BlockSpec((rows, D_MODEL), lambda j: (0, j)),
        out_shape=jax.ShapeDtypeStruct((rows, 3 * D_MODEL), F32),
        compiler_params=pltpu.CompilerParams(dimension_semantics=("arbitrary",)),
        name="adaln",
    )(c_pad, ada_w, ada_b)


def _layer_kernel(x_ref, ada_ref, vec_ref, wmain_ref, wsm_ref, w2_ref, wout_ref,
                  out_ref,
                  h_s, sm_s, xbc_s, u_s, b_s, qd_s, kd_s, ke_s, dec_s, v_s, gact_s, zact_s, mix_s,
                  gstate_s, sstate_s, w2a_s, w2b_s, expand_s):
    tb = TIME_BLOCK
    t = pl.program_id(1)

    @pl.when(t == 0)
    def _():
        gstate_s[...] = jnp.zeros_like(gstate_s)
        sstate_s[...] = jnp.zeros_like(sstate_s)
        xbc_s[0:CONV_PAD, :] = jnp.zeros((CONV_PAD, SSD_CONV_DIM), F32)
        w2_hi, w2_lo = _split(w2_ref[...])
        w2a_s[0:LANES, :] = w2_hi
        w2a_s[LANES:2 * LANES, :] = w2_hi
        w2b_s[...] = w2_lo
        head_row = lax.broadcasted_iota(jnp.int32, (2 * LANES, SSD_INNER), 0) % LANES
        head_of_col = lax.broadcasted_iota(jnp.int32, (2 * LANES, SSD_INNER), 1) // SSD_HEAD_DIM + DT_LANE
        expand_s[...] = (head_row == head_of_col).astype(BF16)

    def vec_view(row, width, rows=1):
        return vec_ref.at[row:row + rows, 0:width]

    prew_ref = vec_view(VEC_PRE_W, D_MODEL)
    gb_ref = vec_view(VEC_GATE_B, GLA_KEY)
    gnw_ref = vec_view(VEC_GLA_NORM, GLA_DV)
    cb_ref = vec_view(VEC_CONV_B, SSD_CONV_DIM)
    cw_ref = vec_view(VEC_CONV_W, SSD_CONV_DIM, SSD_CONV)
    dtb_ref = vec_view(VEC_DT_BIAS, LANES)
    alog_ref = vec_view(VEC_A_LOG, LANES)
    dskip_ref = vec_view(VEC_D_SKIP, SSD_INNER)
    snw_ref = vec_view(VEC_SSD_NORM, SSD_INNER)
    postw_ref = vec_view(VEC_POST_W, D_MODEL)

    ada = ada_ref[pl.ds(pl.program_id(0), 1), :]
    shift = ada[:, 0:D_MODEL]
    scale1 = 1.0 + ada[:, D_MODEL:2 * D_MODEL]
    gate = ada[:, 2 * D_MODEL:3 * D_MODEL]
    causal_g, tril_g = _tril(GLA_CHUNK, BF16)
    tril_g2 = jnp.concatenate([tril_g, tril_g], axis=1)

    def project_tasks(r0):
        sub = slice(r0, r0 + SUB_BLOCK)
        row_chunks = [slice(r, r + ROW_CHUNK) for r in range(r0, r0 + SUB_BLOCK, ROW_CHUNK)]
        tasks = []

        def proj(col, width):
            return _dot(h_s[sub, :], wmain_ref[:, col:col + width])

        def prenorm(rs):
            x = x_ref[0, rs, :]
            ms = jnp.mean(x * x, axis=-1, keepdims=True)
            h = (x * lax.rsqrt(ms + NORM_EPS)) * prew_ref[...] * scale1 + shift
            h_hi = h.astype(BF16)
            h_lo = (h - h_hi.astype(F32)).astype(BF16)
            h_s[rs, :] = h_hi
            both = _dot(h_hi, wsm_ref[...])
            sm_s[rs, :] = both[:, 0:LANES] + both[:, LANES:2 * LANES] + _dot(h_lo, wsm_ref[:, 0:LANES])

        def forget_gate(rs):
            sm_hi, sm_lo = _split(sm_s[rs, :])
            sm_both = jnp.concatenate([sm_hi, sm_lo], axis=1)
            gl = _dot(sm_both, w2a_s[...]) + _dot(sm_hi, w2b_s[...]) + gb_ref[...]
            la = (jnp.minimum(gl, 0.0) - _log1p_exp_neg_abs(gl)) * (1.0 / GLA_TAU)
            for cc in range(ROW_CHUNK // GLA_CHUNK):
                la_hi, la_lo = _split(la[cc * GLA_CHUNK:(cc + 1) * GLA_CHUNK, :])
                b = _dot(tril_g2, jnp.concatenate([la_hi, la_lo], axis=0))
                c0 = rs.start + cc * GLA_CHUNK
                b_s[c0:c0 + GLA_CHUNK, :] = b
                ci = c0 // GLA_CHUNK
                dec_s[ci:ci + 1, :] = jnp.exp(b[GLA_CHUNK - 1:GLA_CHUNK, :])

        def xbc_tile(cs):
            xbc_s[CONV_PAD + r0:CONV_PAD + r0 + SUB_BLOCK, cs] = proj(COL_XBC + cs.start, PROJ_TILE)

        def conv(rs, cs):
            a = xbc_s[rs.start:rs.start + ROW_CHUNK + CONV_PAD, cs]
            acc = cw_ref[0:1, cs] * a
            for kk in range(1, SSD_CONV):
                acc = pltpu.roll(acc, 1, 0) + cw_ref[kk:kk + 1, cs] * a
            u_s[rs, cs] = _silu(acc[CONV_PAD:, :] + cb_ref[:, cs])

        def q_tile():
            q = proj(COL_QK, GLA_KEY)
            qd_s[sub, :] = (q * (GLA_DK ** -0.5) * jnp.exp(b_s[sub, :])).astype(BF16)

        def k_tile():
            k = proj(COL_QK + GLA_KEY, GLA_KEY)
            for cc in range(SUB_BLOCK // GLA_CHUNK):
                rows = slice(r0 + cc * GLA_CHUNK, r0 + (cc + 1) * GLA_CHUNK)
                b = b_s[rows, :]
                k_c = k[cc * GLA_CHUNK:(cc + 1) * GLA_CHUNK, :]
                kd_s[rows, :] = (k_c * jnp.exp(-b)).astype(BF16)
                ke_s[rows, :] = (k_c * jnp.exp(b[GLA_CHUNK - 1:GLA_CHUNK, :] - b)).astype(BF16)

        def act_tile(dst_s, col, cs):
            dst_s[sub, cs] = _silu(proj(col + cs.start, PROJ_TILE)).astype(BF16)

        def v_tile(cs):
            v_s[sub, cs] = proj(COL_V + cs.start, PROJ_TILE).astype(BF16)

        for rs in row_chunks:
            tasks.append(lambda rs=rs: prenorm(rs))
        for rs in row_chunks:
            tasks.append(lambda rs=rs: forget_gate(rs))
        for col in range(0, SSD_CONV_DIM, PROJ_TILE):
            cs = slice(col, col + PROJ_TILE)
            tasks.append(lambda cs=cs: xbc_tile(cs))
            for rs in row_chunks:
                tasks.append(lambda rs=rs, cs=cs: conv(rs, cs))
        tasks.append(q_tile)
        tasks.append(k_tile)
        for col in range(0, GLA_VAL, PROJ_TILE):
            cs = slice(col, col + PROJ_TILE)
            tasks.append(lambda cs=cs: act_tile(gact_s, COL_G, cs))
        for col in range(0, SSD_INNER, PROJ_TILE):
            cs = slice(col, col + PROJ_TILE)
            tasks.append(lambda cs=cs: act_tile(zact_s, COL_Z, cs))
        for col in range(0, GLA_VAL, PROJ_TILE):
            cs = slice(col, col + PROJ_TILE)
            tasks.append(lambda cs=cs: v_tile(cs))
        return tasks

    def gla_chunk(ci):
        rows = slice(ci * GLA_CHUNK, (ci + 1) * GLA_CHUNK)
        decay = dec_s[ci:ci + 1, :]
        for hh in range(GLA_HEADS):
            ks = slice(hh * GLA_DK, (hh + 1) * GLA_DK)
            vs = slice(hh * GLA_DV, (hh + 1) * GLA_DV)
            q_dec = qd_s[rows, ks]
            att = _dot_nt(q_dec, kd_s[rows, ks])
            att = jnp.where(causal_g, att, 0.0).astype(BF16)
            v_h = v_s[rows, vs]
            state_t = gstate_s[hh]
            o = _dot(att, v_h) + _dot_nt(q_dec, state_t.astype(BF16))
            gstate_s[hh] = state_t * decay[:, ks] + _dot_tn(v_h, ke_s[rows, ks])
            o = (o * lax.rsqrt(jnp.mean(o * o, axis=-1, keepdims=True) + NORM_EPS)) * gnw_ref[...]
            mix_s[rows, vs] = (o * gact_s[rows, vs].astype(F32)).astype(BF16)

    causal_s, tril_s = _tril(SSD_CHUNK, BF16)
    tril_s2 = jnp.concatenate([tril_s, tril_s], axis=1)
    a_neg = -jnp.exp(alog_ref[...])

    def expand_heads(v):
        hi, lo = _split(v)
        return _dot(jnp.concatenate([hi, lo], axis=1), expand_s[...])

    def ssd_decays(ci, shared):
        rows = slice(ci * SSD_CHUNK, (ci + 1) * SSD_CHUNK)
        xs = u_s[rows, 0:SSD_INNER]
        dt_in = sm_s[rows, :] + dtb_ref[...]
        dt = jnp.maximum(dt_in, 0.0) + _log1p_exp_neg_abs(dt_in)
        dta_hi, dta_lo = _split(dt * a_neg)
        a_cum = _dot(tril_s2, jnp.concatenate([dta_hi, dta_lo], axis=0))
        a_last = a_cum[SSD_CHUNK - 1:SSD_CHUNK, :]
        decay_in_x = expand_heads(jnp.exp(a_cum))
        decay_end_dt_x = expand_heads(jnp.exp(a_last - a_cum) * dt)
        shared.update(
            a_cum=a_cum, a_cum_t=a_cum.T, dt_t=dt.T,
            decay_in_x=decay_in_x, chunk_decay_x=decay_in_x[SSD_CHUNK - 1:SSD_CHUNK, :],
            xs_scaled=(xs * decay_end_dt_x).astype(BF16))

    def ssd_group(ci, g, shared):
        rows = slice(ci * SSD_CHUNK, (ci + 1) * SSD_CHUNK)
        gs = slice(g * SSD_GROUP_WIDTH, (g + 1) * SSD_GROUP_WIDTH)
        b_col = SSD_INNER + g * SSD_STATE
        c_col = SSD_INNER + SSD_GROUPS * SSD_STATE + g * SSD_STATE
        xs = u_s[rows, gs]
        xs_b = xs.astype(BF16)
        b_g = u_s[rows, b_col:b_col + SSD_STATE].astype(BF16)
        c_g = u_s[rows, c_col:c_col + SSD_STATE].astype(BF16)
        a_cum, a_cum_t, dt_t = shared["a_cum"], shared["a_cum_t"], shared["dt_t"]
        cb = _dot_nt(c_g, b_g)
        state = sstate_s[g]
        y_g = _dot(c_g, state.astype(BF16)) * shared["decay_in_x"][:, gs]
        sstate_s[g] = state * shared["chunk_decay_x"][:, gs] + _dot_tn(b_g, shared["xs_scaled"][:, gs])
        y_heads = []
        for hl in range(SSD_HEADS_PER_GROUP):
            lane = DT_LANE + g * SSD_HEADS_PER_GROUP + hl
            seg = a_cum[:, lane:lane + 1] - a_cum_t[lane:lane + 1, :]
            w = jnp.where(causal_s, jnp.exp(seg), 0.0) * cb * dt_t[lane:lane + 1, :]
            y_heads.append(_dot(w.astype(BF16), xs_b[:, hl * SSD_HEAD_DIM:(hl + 1) * SSD_HEAD_DIM]))
        y_g = y_g + jnp.concatenate(y_heads, axis=1) + dskip_ref[:, gs] * xs
        y_g = y_g * zact_s[rows, gs].astype(F32)
        y_g = (y_g * lax.rsqrt(jnp.mean(y_g * y_g, axis=-1, keepdims=True) + NORM_EPS)) * snw_ref[:, gs]
        mix_s[rows, GLA_VAL + g * SSD_GROUP_WIDTH:GLA_VAL + (g + 1) * SSD_GROUP_WIDTH] = y_g.astype(BF16)

    def out_rows(rs):
        mixed = _dot(mix_s[rs, :], wout_ref[...])
        mixed = (mixed * lax.rsqrt(jnp.mean(mixed * mixed, axis=-1, keepdims=True) + NORM_EPS)) * postw_ref[...]
        out_ref[0, rs, :] = x_ref[0, rs, :] + gate * mixed

    def recur_tasks(r0):
        tasks = []
        gla_per_ssd = SSD_CHUNK // GLA_CHUNK
        for ci in range(r0 // SSD_CHUNK, (r0 + SUB_BLOCK) // SSD_CHUNK):
            shared = {}
            tasks.append(lambda ci=ci, shared=shared: ssd_decays(ci, shared))
            for cc in range(gla_per_ssd):
                tasks.append(lambda ci=ci, cc=cc: gla_chunk(ci * gla_per_ssd + cc))
            for g in range(SSD_GROUPS):
                tasks.append(lambda ci=ci, g=g, shared=shared: ssd_group(ci, g, shared))
        tasks.append(lambda: out_rows(slice(r0, r0 + SUB_BLOCK)))
        return tasks

    def issue_alternating(a, b):
        order = sorted([((i + 0.5) / len(a), 0, i) for i in range(len(a))]
                       + [((i + 0.5) / len(b), 1, i) for i in range(len(b))])
        for _, which, i in order:
            (a, b)[which][i]()

    sub_starts = list(range(0, tb, SUB_BLOCK))
    for task in project_tasks(sub_starts[0]):
        task()
    for prev, cur in zip(sub_starts[:-1], sub_starts[1:]):
        issue_alternating(project_tasks(cur), recur_tasks(prev))
    for task in recur_tasks(sub_starts[-1]):
        task()

    xbc_s[0:CONV_PAD, :] = xbc_s[tb:tb + CONV_PAD, :]


def _const_spec(shape):
    zeros = (0,) * len(shape)
    return pl.BlockSpec(shape, lambda b, t: zeros, pipeline_mode=pl.Buffered(1))


def _layer(x, ada, vecs, wmain, wsm, w2, wout):
    bsz, seq, _ = x.shape
    tb = TIME_BLOCK
    consts = (ada, vecs, wmain, wsm, w2, wout)
    return pl.pallas_call(
        _layer_kernel,
        grid=(bsz, seq // tb),
        in_specs=[pl.BlockSpec((1, tb, D_MODEL), lambda b, t: (b, t, 0))]
        + [_const_spec(a.shape) for a in consts],
        out_specs=pl.BlockSpec((1, tb, D_MODEL), lambda b, t: (b, t, 0)),
        out_shape=jax.ShapeDtypeStruct(x.shape, x.dtype),
        scratch_shapes=[
            pltpu.VMEM((tb, D_MODEL), BF16),
            pltpu.VMEM((tb, LANES), F32),
            pltpu.VMEM((tb + CONV_PAD, SSD_CONV_DIM), F32),
            pltpu.VMEM((tb, SSD_CONV_DIM), F32),
            pltpu.VMEM((tb, GLA_KEY), F32),
            pltpu.VMEM((tb, GLA_KEY), BF16),
            pltpu.VMEM((tb, GLA_KEY), BF16),
            pltpu.VMEM((tb, GLA_KEY), BF16),
            pltpu.VMEM((tb // GLA_CHUNK, GLA_KEY), F32),
            pltpu.VMEM((tb, GLA_VAL), BF16),
            pltpu.VMEM((tb, GLA_VAL), BF16),
            pltpu.VMEM((tb, SSD_INNER), BF16),
            pltpu.VMEM((tb, D_MIX), BF16),
            pltpu.VMEM((GLA_HEADS, GLA_DV, GLA_DK), F32),
            pltpu.VMEM((SSD_GROUPS, SSD_STATE, SSD_GROUP_WIDTH), F32),
            pltpu.VMEM((2 * LANES, GLA_KEY), BF16),
            pltpu.VMEM((LANES, GLA_KEY), BF16),
            pltpu.VMEM((2 * LANES, SSD_INNER), BF16),
        ],
        compiler_params=pltpu.CompilerParams(
            dimension_semantics=("arbitrary", "arbitrary"),
            vmem_limit_bytes=VMEM_LIMIT_BYTES),
        name="hybrid_layer",
    )(x, *consts)


def _prep_kernel(w_ref, wmain_ref, wsm_ref):
    wmain_ref[:, 0:IN_GATE] = w_ref[0, :, 0:IN_GATE].astype(BF16)
    wmain_ref[:, IN_GATE:W_MAIN_COLS] = w_ref[0, :, IN_Z:IN_DT].astype(BF16)
    lane = lax.broadcasted_iota(jnp.int32, (PREP_ROWS, LANES), 1)
    first = w_ref[0, :, IN_GATE:IN_GATE + LANES]
    last_start = IN_DT - DT_LANE
    last = jnp.concatenate(
        [w_ref[0, :, last_start:D_PROJ], jnp.zeros((PREP_ROWS, LANES - (D_PROJ - last_start)), F32)], axis=1)
    wsm = jnp.where(lane < GLA_GATE_RANK, first, jnp.where(lane < DT_LANE + SSD_HEADS, last, 0.0))
    hi, lo = _split(wsm)
    wsm_ref[:, 0:LANES] = hi
    wsm_ref[:, LANES:2 * LANES] = lo


def _prep_weights(w_in_i):
    return pl.pallas_call(
        _prep_kernel,
        grid=(D_MODEL // PREP_ROWS,),
        in_specs=[pl.BlockSpec((1, PREP_ROWS, D_PROJ), lambda r: (0, r, 0))],
        out_specs=[pl.BlockSpec((PREP_ROWS, W_MAIN_COLS), lambda r: (r, 0)),
                   pl.BlockSpec((PREP_ROWS, 2 * LANES), lambda r: (r, 0))],
        out_shape=[jax.ShapeDtypeStruct((D_MODEL, W_MAIN_COLS), BF16),
                   jax.ShapeDtypeStruct((D_MODEL, 2 * LANES), BF16)],
        compiler_params=pltpu.CompilerParams(dimension_semantics=("arbitrary",)),
        name="prep_weights",
    )(w_in_i)


def _row(v, offset=0):
    return jnp.pad(v.astype(F32), (offset, VEC_COLS - offset - v.shape[0]))[None, :]


def kernel(x, c, ada_w, ada_b, pre_norm_w, w_in, gla_gate_w2, gla_gate_b, gla_norm_w, conv_w, conv_b,
           dt_bias, a_log, d_skip, ssd_norm_w, w_out, post_norm_w):
    bsz, seq, _ = x.shape
    assert seq % TIME_BLOCK == 0 and bsz <= SUBLANES
    depth = ada_w.shape[0]
    c_pad = jnp.pad(c, ((0, SUBLANES - bsz), (0, 0)))
    for i in range(depth):
        ada = _ada(c_pad, ada_w[i], ada_b[i][None, :])
        wmain, wsm2 = _prep_weights(w_in[i:i + 1])
        vecs = jnp.concatenate(
            [_row(pre_norm_w[i]), _row(gla_gate_b[i]), _row(gla_norm_w[i]), _row(conv_b[i])]
            + [_row(conv_w[i, kk]) for kk in range(SSD_CONV)]
            + [_row(dt_bias[i], DT_LANE), _row(a_log[i], DT_LANE),
               _row(jnp.repeat(d_skip[i], SSD_HEAD_DIM)), _row(ssd_norm_w[i]), _row(post_norm_w[i])]
            + [jnp.zeros((VEC_ROWS - VEC_USED, VEC_COLS), F32)], axis=0)
        w2 = jnp.pad(gla_gate_w2[i], ((0, LANES - GLA_GATE_RANK), (0, 0)))
        x = _layer(x, ada, vecs, wmain, wsm2, w2, w_out[i].astype(BF16))
    return x
```

```python
import jax
import jax.numpy as jnp
from jax import lax
from jax.experimental import pallas as pl
from jax.experimental.pallas import tpu as pltpu

F32 = jnp.float32
BF16 = jnp.bfloat16

NORM_EPS = 1e-6
D_MODEL = 1024
GLA_HEADS = 4
GLA_DK = 128
GLA_DV = 256
GLA_KEY = GLA_HEADS * GLA_DK
GLA_VAL = GLA_HEADS * GLA_DV
GLA_GATE_RANK = 16
GLA_TAU = 16.0
GLA_CHUNK = 64
SSD_INNER = 1024
SSD_HEAD_DIM = 64
SSD_HEADS = 16
SSD_GROUPS = 2
SSD_HEADS_PER_GROUP = SSD_HEADS // SSD_GROUPS
SSD_GROUP_WIDTH = SSD_INNER // SSD_GROUPS
SSD_STATE = 128
SSD_CONV = 4
SSD_CHUNK = 128
SSD_CONV_DIM = SSD_INNER + 2 * SSD_GROUPS * SSD_STATE
D_MIX = GLA_VAL + SSD_INNER

LANES = 128
SUBLANES = 8

TIME_BLOCK = 512
ROW_CHUNK = 128
PROJ_TILE = 512
SUB_BLOCK = 256
COL_QK = 0
COL_V = COL_QK + 2 * GLA_KEY
COL_G = COL_V + GLA_VAL
COL_Z = COL_G + GLA_VAL
COL_XBC = COL_Z + SSD_INNER
W_MAIN_COLS = COL_XBC + SSD_CONV_DIM
DT_LANE = GLA_GATE_RANK
assert DT_LANE + SSD_HEADS <= LANES
IN_GATE = 2 * GLA_KEY + 2 * GLA_VAL
IN_Z = IN_GATE + GLA_GATE_RANK
IN_DT = IN_Z + SSD_INNER + SSD_CONV_DIM
D_PROJ = IN_DT + SSD_HEADS
assert IN_GATE % PROJ_TILE == 0 and W_MAIN_COLS % PROJ_TILE == 0
(VEC_PRE_W, VEC_GATE_B, VEC_GLA_NORM, VEC_CONV_B, VEC_CONV_W, VEC_DT_BIAS, VEC_A_LOG,
 VEC_D_SKIP, VEC_SSD_NORM, VEC_POST_W) = (0, 1, 2, 3, 4, 4 + SSD_CONV, 5 + SSD_CONV, 6 + SSD_CONV,
                                          7 + SSD_CONV, 8 + SSD_CONV)
VEC_USED = VEC_POST_W + 1
VEC_ROWS = -(-VEC_USED // SUBLANES) * SUBLANES
VEC_COLS = SSD_CONV_DIM
CONV_PAD = SUBLANES
VMEM_LIMIT_BYTES = 56 * 1024 * 1024


def _dot(a, b):
    return jnp.dot(a, b, preferred_element_type=F32)


def _dot_nt(a, b):
    return lax.dot_general(a, b, (((1,), (1,)), ((), ())), preferred_element_type=F32)


def _dot_tn(a, b):
    return lax.dot_general(a, b, (((0,), (0,)), ((), ())), preferred_element_type=F32)


def _split(x):
    hi = x.astype(BF16)
    lo = (x - hi.astype(F32)).astype(BF16)
    return hi, lo


def _silu(x):
    hx = 0.5 * x
    return hx + hx * jnp.tanh(hx)


def _log1p_exp_neg_abs(x):
    return jnp.log(1.0 + jnp.exp(-jnp.abs(x)))


def _tril(n, dtype):
    r = lax.broadcasted_iota(jnp.int32, (n, n), 0)
    c = lax.broadcasted_iota(jnp.int32, (n, n), 1)
    return r >= c, (r >= c).astype(dtype)


def _ada_kernel(c_ref, w_ref, b_ref, o_ref):
    c = c_ref[...]
    ca_hi, ca_lo = _split(_silu(c))
    w_hi, w_lo = _split(w_ref[...])
    o_ref[...] = _dot(ca_hi, w_hi) + _dot(ca_hi, w_lo) + _dot(ca_lo, w_hi) + b_ref[...]


def _ada(c_pad, ada_w, ada_b):
    rows = c_pad.shape[0]
    return pl.pallas_call(
        _ada_kernel,
        grid=(3,),
        in_specs=[
            pl.BlockSpec((rows, D_MODEL), lambda j: (0, 0)),
            pl.BlockSpec((D_MODEL, D_MODEL), lambda j: (0, j)),
            pl.BlockSpec((1, D_MODEL), lambda j: (0, j)),
        ],
        out_specs=pl.BlockSpec((rows, D_MODEL), lambda j: (0, j)),
        out_shape=jax.ShapeDtypeStruct((rows, 3 * D_MODEL), F32),
        compiler_params=pltpu.CompilerParams(dimension_semantics=("arbitrary",)),
        name="adaln",
    )(c_pad, ada_w, ada_b)


def _layer_kernel(x_ref, ada_ref, vec_ref, wmain_ref, wsm_ref, w2_ref, wout_ref,
                  out_ref,
                  h_s, sm_s, xbc_s, u_s, b_s, qd_s, kd_s, ke_s, dec_s, v_s, gact_s, zact_s, mix_s,
                  gstate_s, sstate_s, w2a_s, w2b_s, expand_s):
    tb = TIME_BLOCK
    t = pl.program_id(1)

    @pl.when(t == 0)
    def _():
        gstate_s[...] = jnp.zeros_like(gstate_s)
        sstate_s[...] = jnp.zeros_like(sstate_s)
        xbc_s[0:CONV_PAD, :] = jnp.zeros((CONV_PAD, SSD_CONV_DIM), F32)
        w2_hi, w2_lo = _split(w2_ref[...])
        w2a_s[0:LANES, :] = w2_hi
        w2a_s[LANES:2 * LANES, :] = w2_hi
        w2b_s[...] = w2_lo
        head_row = lax.broadcasted_iota(jnp.int32, (2 * LANES, SSD_INNER), 0) % LANES
        head_of_col = lax.broadcasted_iota(jnp.int32, (2 * LANES, SSD_INNER), 1) // SSD_HEAD_DIM + DT_LANE
        expand_s[...] = (head_row == head_of_col).astype(BF16)

    def vec_view(row, width, rows=1):
        return vec_ref.at[row:row + rows, 0:width]

    prew_ref = vec_view(VEC_PRE_W, D_MODEL)
    gb_ref = vec_view(VEC_GATE_B, GLA_KEY)
    gnw_ref = vec_view(VEC_GLA_NORM, GLA_DV)
    cb_ref = vec_view(VEC_CONV_B, SSD_CONV_DIM)
    cw_ref = vec_view(VEC_CONV_W, SSD_CONV_DIM, SSD_CONV)
    dtb_ref = vec_view(VEC_DT_BIAS, LANES)
    alog_ref = vec_view(VEC_A_LOG, LANES)
    dskip_ref = vec_view(VEC_D_SKIP, SSD_INNER)
    snw_ref = vec_view(VEC_SSD_NORM, SSD_INNER)
    postw_ref = vec_view(VEC_POST_W, D_MODEL)

    ada = ada_ref[pl.ds(pl.program_id(0), 1), :]
    shift = ada[:, 0:D_MODEL]
    scale1 = 1.0 + ada[:, D_MODEL:2 * D_MODEL]
    gate = ada[:, 2 * D_MODEL:3 * D_MODEL]
    causal_g, tril_g = _tril(GLA_CHUNK, BF16)
    tril_g2 = jnp.concatenate([tril_g, tril_g], axis=1)

    def project_tasks(r0):
        sub = slice(r0, r0 + SUB_BLOCK)
        row_chunks = [slice(r, r + ROW_CHUNK) for r in range(r0, r0 + SUB_BLOCK, ROW_CHUNK)]
        tasks = []

        def proj(col, width):
            return _dot(h_s[sub, :], wmain_ref[:, col:col + width])

        def prenorm(rs):
            x = x_ref[0, rs, :]
            ms = jnp.mean(x * x, axis=-1, keepdims=True)
            h = (x * lax.rsqrt(ms + NORM_EPS)) * prew_ref[...] * scale1 + shift
            h_hi = h.astype(BF16)
            h_lo = (h - h_hi.astype(F32)).astype(BF16)
            h_s[rs, :] = h_hi
            both = _dot(h_hi, wsm_ref[...])
            sm_s[rs, :] = both[:, 0:LANES] + both[:, LANES:2 * LANES] + _dot(h_lo, wsm_ref[:, 0:LANES])

        def forget_gate(rs):
            sm_hi, sm_lo = _split(sm_s[rs, :])
            sm_both = jnp.concatenate([sm_hi, sm_lo], axis=1)
            gl = _dot(sm_both, w2a_s[...]) + _dot(sm_hi, w2b_s[...]) + gb_ref[...]
            la = (jnp.minimum(gl, 0.0) - _log1p_exp_neg_abs(gl)) * (1.0 / GLA_TAU)
            for cc in range(ROW_CHUNK // GLA_CHUNK):
                la_hi, la_lo = _split(la[cc * GLA_CHUNK:(cc + 1) * GLA_CHUNK, :])
                b = _dot(tril_g2, jnp.concatenate([la_hi, la_lo], axis=0))
                c0 = rs.start + cc * GLA_CHUNK
                b_s[c0:c0 + GLA_CHUNK, :] = b
                ci = c0 // GLA_CHUNK
                dec_s[ci:ci + 1, :] = jnp.exp(b[GLA_CHUNK - 1:GLA_CHUNK, :])

        def xbc_tile(cs):
            xbc_s[CONV_PAD + r0:CONV_PAD + r0 + SUB_BLOCK, cs] = proj(COL_XBC + cs.start, PROJ_TILE)

        def conv(rs, cs):
            a = xbc_s[rs.start:rs.start + ROW_CHUNK + CONV_PAD, cs]
            acc = cw_ref[0:1, cs] * a
            for kk in range(1, SSD_CONV):
                acc = pltpu.roll(acc, 1, 0) + cw_ref[kk:kk + 1, cs] * a
            u_s[rs, cs] = _silu(acc[CONV_PAD:, :] + cb_ref[:, cs])

        def q_tile():
            q = proj(COL_QK, GLA_KEY)
            qd_s[sub, :] = (q * (GLA_DK ** -0.5) * jnp.exp(b_s[sub, :])).astype(BF16)

        def k_tile():
            k = proj(COL_QK + GLA_KEY, GLA_KEY)
            for cc in range(SUB_BLOCK // GLA_CHUNK):
                rows = slice(r0 + cc * GLA_CHUNK, r0 + (cc + 1) * GLA_CHUNK)
                b = b_s[rows, :]
                k_c = k[cc * GLA_CHUNK:(cc + 1) * GLA_CHUNK, :]
                kd_s[rows, :] = (k_c * jnp.exp(-b)).astype(BF16)
                ke_s[rows, :] = (k_c * jnp.exp(b[GLA_CHUNK - 1:GLA_CHUNK, :] - b)).astype(BF16)

        def act_tile(dst_s, col, cs):
            dst_s[sub, cs] = _silu(proj(col + cs.start, PROJ_TILE)).astype(BF16)

        def v_tile(cs):
            v_s[sub, cs] = proj(COL_V + cs.start, PROJ_TILE).astype(BF16)

        for rs in row_chunks:
            tasks.append(lambda rs=rs: prenorm(rs))
        for rs in row_chunks:
            tasks.append(lambda rs=rs: forget_gate(rs))
        for col in range(0, SSD_CONV_DIM, PROJ_TILE):
            cs = slice(col, col + PROJ_TILE)
            tasks.append(lambda cs=cs: xbc_tile(cs))
            for rs in row_chunks:
                tasks.append(lambda rs=rs, cs=cs: conv(rs, cs))
        tasks.append(q_tile)
        tasks.append(k_tile)
        for col in range(0, GLA_VAL, PROJ_TILE):
            cs = slice(col, col + PROJ_TILE)
            tasks.append(lambda cs=cs: act_tile(gact_s, COL_G, cs))
        for col in range(0, SSD_INNER, PROJ_TILE):
            cs = slice(col, col + PROJ_TILE)
            tasks.append(lambda cs=cs: act_tile(zact_s, COL_Z, cs))
        for col in range(0, GLA_VAL, PROJ_TILE):
            cs = slice(col, col + PROJ_TILE)
            tasks.append(lambda cs=cs: v_tile(cs))
        return tasks

    def gla_chunk(ci):
        rows = slice(ci * GLA_CHUNK, (ci + 1) * GLA_CHUNK)
        decay = dec_s[ci:ci + 1, :]
        for hh in range(GLA_HEADS):
            ks = slice(hh * GLA_DK, (hh + 1) * GLA_DK)
            vs = slice(hh * GLA_DV, (hh + 1) * GLA_DV)
            q_dec = qd_s[rows, ks]
            att = _dot_nt(q_dec, kd_s[rows, ks])
            att = jnp.where(causal_g, att, 0.0).astype(BF16)
            v_h = v_s[rows, vs]
            state_t = gstate_s[hh]
            o = _dot(att, v_h) + _dot_nt(q_dec, state_t.astype(BF16))
            gstate_s[hh] = state_t * decay[:, ks] + _dot_tn(v_h, ke_s[rows, ks])
            o = (o * lax.rsqrt(jnp.mean(o * o, axis=-1, keepdims=True) + NORM_EPS)) * gnw_ref[...]
            mix_s[rows, vs] = (o * gact_s[rows, vs].astype(F32)).astype(BF16)

    causal_s, tril_s = _tril(SSD_CHUNK, BF16)
    tril_s2 = jnp.concatenate([tril_s, tril_s], axis=1)
    a_neg = -jnp.exp(alog_ref[...])

    def expand_heads(v):
        hi, lo = _split(v)
        return _dot(jnp.concatenate([hi, lo], axis=1), expand_s[...])

    def ssd_decays(ci, shared):
        rows = slice(ci * SSD_CHUNK, (ci + 1) * SSD_CHUNK)
        xs = u_s[rows, 0:SSD_INNER]
        dt_in = sm_s[rows, :] + dtb_ref[...]
        dt = jnp.maximum(dt_in, 0.0) + _log1p_exp_neg_abs(dt_in)
        dta_hi, dta_lo = _split(dt * a_neg)
        a_cum = _dot(tril_s2, jnp.concatenate([dta_hi, dta_lo], axis=0))
        a_last = a_cum[SSD_CHUNK - 1:SSD_CHUNK, :]
        decay_in_x = expand_heads(jnp.exp(a_cum))
        decay_end_dt_x = expand_heads(jnp.exp(a_last - a_cum) * dt)
        shared.update(
            a_cum=a_cum, a_cum_t=a_cum.T, dt_t=dt.T,
            decay_in_x=decay_in_x, chunk_decay_x=decay_in_x[SSD_CHUNK - 1:SSD_CHUNK, :],
            xs_scaled=(xs * decay_end_dt_x).astype(BF16))

    def ssd_group(ci, g, shared):
        rows = slice(ci * SSD_CHUNK, (ci + 1) * SSD_CHUNK)
        gs = slice(g * SSD_GROUP_WIDTH, (g + 1) * SSD_GROUP_WIDTH)
        b_col = SSD_INNER + g * SSD_STATE
        c_col = SSD_INNER + SSD_GROUPS * SSD_STATE + g * SSD_STATE
        xs = u_s[rows, gs]
        xs_b = xs.astype(BF16)
        b_g = u_s[rows, b_col:b_col + SSD_STATE].astype(BF16)
        c_g = u_s[rows, c_col:c_col + SSD_STATE].astype(BF16)
        a_cum, a_cum_t, dt_t = shared["a_cum"], shared["a_cum_t"], shared["dt_t"]
        cb = _dot_nt(c_g, b_g)
        state = sstate_s[g]
        y_g = _dot(c_g, state.astype(BF16)) * shared["decay_in_x"][:, gs]
        sstate_s[g] = state * shared["chunk_decay_x"][:, gs] + _dot_tn(b_g, shared["xs_scaled"][:, gs])
        y_heads = []
        for hl in range(SSD_HEADS_PER_GROUP):
            lane = DT_LANE + g * SSD_HEADS_PER_GROUP + hl
            seg = a_cum[:, lane:lane + 1] - a_cum_t[lane:lane + 1, :]
            w = jnp.where(causal_s, jnp.exp(seg), 0.0) * cb * dt_t[lane:lane + 1, :]
            y_heads.append(_dot(w.astype(BF16), xs_b[:, hl * SSD_HEAD_DIM:(hl + 1) * SSD_HEAD_DIM]))
        y_g = y_g + jnp.concatenate(y_heads, axis=1) + dskip_ref[:, gs] * xs
        y_g = y_g * zact_s[rows, gs].astype(F32)
        y_g = (y_g * lax.rsqrt(jnp.mean(y_g * y_g, axis=-1, keepdims=True) + NORM_EPS)) * snw_ref[:, gs]
        mix_s[rows, GLA_VAL + g * SSD_GROUP_WIDTH:GLA_VAL + (g + 1) * SSD_GROUP_WIDTH] = y_g.astype(BF16)

    def out_rows(rs):
        mixed = _dot(mix_s[rs, :], wout_ref[...])
        mixed = (mixed * lax.rsqrt(jnp.mean(mixed * mixed, axis=-1, keepdims=True) + NORM_EPS)) * postw_ref[...]
        out_ref[0, rs, :] = x_ref[0, rs, :] + gate * mixed

    def recur_tasks(r0):
        tasks = []
        gla_per_ssd = SSD_CHUNK // GLA_CHUNK
        for ci in range(r0 // SSD_CHUNK, (r0 + SUB_BLOCK) // SSD_CHUNK):
            shared = {}
            tasks.append(lambda ci=ci, shared=shared: ssd_decays(ci, shared))
            for cc in range(gla_per_ssd):
                tasks.append(lambda ci=ci, cc=cc: gla_chunk(ci * gla_per_ssd + cc))
            for g in range(SSD_GROUPS):
                tasks.append(lambda ci=ci, g=g, shared=shared: ssd_group(ci, g, shared))
        tasks.append(lambda: out_rows(slice(r0, r0 + SUB_BLOCK)))
        return tasks

    def issue_alternating(a, b):
        order = sorted([((i + 0.5) / len(a), 0, i) for i in range(len(a))]
                       + [((i + 0.5) / len(b), 1, i) for i in range(len(b))])
        for _, which, i in order:
            (a, b)[which][i]()

    sub_starts = list(range(0, tb, SUB_BLOCK))
    for task in project_tasks(sub_starts[0]):
        task()
    for prev, cur in zip(sub_starts[:-1], sub_starts[1:]):
        issue_alternating(project_tasks(cur), recur_tasks(prev))
    for task in recur_tasks(sub_starts[-1]):
        task()

    xbc_s[0:CONV_PAD, :] = xbc_s[tb:tb + CONV_PAD, :]


def _const_spec(shape):
    zeros = (0,) * len(shape)
    return pl.BlockSpec(shape, lambda b, t: zeros, pipeline_mode=pl.Buffered(1))


def _layer(x, ada, vecs, wmain, wsm, w2, wout):
    bsz, seq, _ = x.shape
    tb = TIME_BLOCK
    consts = (ada, vecs, wmain, wsm, w2, wout)
    return pl.pallas_call(
        _layer_kernel,
        grid=(bsz, seq // tb),
        in_specs=[pl.BlockSpec((1, tb, D_MODEL), lambda b, t: (b, t, 0))]
        + [_const_spec(a.shape) for a in consts],
        out_specs=pl.BlockSpec((1, tb, D_MODEL), lambda b, t: (b, t, 0)),
        out_shape=jax.ShapeDtypeStruct(x.shape, x.dtype),
        scratch_shapes=[
            pltpu.VMEM((tb, D_MODEL), BF16),
            pltpu.VMEM((tb, LANES), F32),
            pltpu.VMEM((tb + CONV_PAD, SSD_CONV_DIM), F32),
            pltpu.VMEM((tb, SSD_CONV_DIM), F32),
            pltpu.VMEM((tb, GLA_KEY), F32),
            pltpu.VMEM((tb, GLA_KEY), BF16),
            pltpu.VMEM((tb, GLA_KEY), BF16),
            pltpu.VMEM((tb, GLA_KEY), BF16),
            pltpu.VMEM((tb // GLA_CHUNK, GLA_KEY), F32),
            pltpu.VMEM((tb, GLA_VAL), BF16),
            pltpu.VMEM((tb, GLA_VAL), BF16),
            pltpu.VMEM((tb, SSD_INNER), BF16),
            pltpu.VMEM((tb, D_MIX), BF16),
            pltpu.VMEM((GLA_HEADS, GLA_DV, GLA_DK), F32),
            pltpu.VMEM((SSD_GROUPS, SSD_STATE, SSD_GROUP_WIDTH), F32),
            pltpu.VMEM((2 * LANES, GLA_KEY), BF16),
            pltpu.VMEM((LANES, GLA_KEY), BF16),
            pltpu.VMEM((2 * LANES, SSD_INNER), BF16),
        ],
        compiler_params=pltpu.CompilerParams(
            dimension_semantics=("arbitrary", "arbitrary"),
            vmem_limit_bytes=VMEM_LIMIT_BYTES),
        name="hybrid_layer",
    )(x, *consts)


def _prep_kernel(wt_ref, gate_ref, dt_ref, wmain_ref, wsm_ref):
    wmain_ref[...] = wt_ref[...].T.astype(BF16)

    @pl.when(pl.program_id(0) == 0)
    def _():
        narrow_t = jnp.concatenate(
            [gate_ref[...], dt_ref[...], jnp.zeros((LANES - DT_LANE - SSD_HEADS, D_MODEL), F32)], axis=0)
        hi, lo = _split(narrow_t.T)
        wsm_ref[:, 0:LANES] = hi
        wsm_ref[:, LANES:2 * LANES] = lo


def _prep_weights(w_t):
    def src_row(j):
        return pl.multiple_of(j * PROJ_TILE + jnp.where(j * PROJ_TILE >= IN_GATE, GLA_GATE_RANK, 0), SUBLANES)

    return pl.pallas_call(
        _prep_kernel,
        grid=(W_MAIN_COLS // PROJ_TILE,),
        in_specs=[pl.BlockSpec((pl.Element(PROJ_TILE), pl.Element(D_MODEL)), lambda j: (src_row(j), 0)),
                  pl.BlockSpec((pl.Element(GLA_GATE_RANK), pl.Element(D_MODEL)), lambda j: (IN_GATE, 0)),
                  pl.BlockSpec((pl.Element(SSD_HEADS), pl.Element(D_MODEL)), lambda j: (IN_DT, 0))],
        out_specs=[pl.BlockSpec((D_MODEL, PROJ_TILE), lambda j: (0, j)),
                   pl.BlockSpec((D_MODEL, 2 * LANES), lambda j: (0, 0))],
        out_shape=[jax.ShapeDtypeStruct((D_MODEL, W_MAIN_COLS), BF16),
                   jax.ShapeDtypeStruct((D_MODEL, 2 * LANES), BF16)],
        compiler_params=pltpu.CompilerParams(dimension_semantics=("arbitrary",)),
        name="prep_weights",
    )(w_t, w_t, w_t)


def _row(v, offset=0):
    return jnp.pad(v.astype(F32), (offset, VEC_COLS - offset - v.shape[0]))[None, :]


def kernel(x, c, ada_w, ada_b, pre_norm_w, w_in, gla_gate_w2, gla_gate_b, gla_norm_w, conv_w, conv_b,
           dt_bias, a_log, d_skip, ssd_norm_w, w_out, post_norm_w):
    bsz, seq, _ = x.shape
    assert seq % TIME_BLOCK == 0 and bsz <= SUBLANES
    depth = ada_w.shape[0]
    c_pad = jnp.pad(c, ((0, SUBLANES - bsz), (0, 0)))
    for i in range(depth):
        ada = _ada(c_pad, ada_w[i], ada_b[i][None, :])
        wmain, wsm2 = _prep_weights(w_in[i].T)
        vecs = jnp.concatenate(
            [_row(pre_norm_w[i]), _row(gla_gate_b[i]), _row(gla_norm_w[i]), _row(conv_b[i])]
            + [_row(conv_w[i, kk]) for kk in range(SSD_CONV)]
            + [_row(dt_bias[i], DT_LANE), _row(a_log[i], DT_LANE),
               _row(jnp.repeat(d_skip[i], SSD_HEAD_DIM)), _row(ssd_norm_w[i]), _row(post_norm_w[i])]
            + [jnp.zeros((VEC_ROWS - VEC_USED, VEC_COLS), F32)], axis=0)
        w2 = jnp.pad(gla_gate_w2[i], ((0, LANES - GLA_GATE_RANK), (0, 0)))
        x = _layer(x, ada, vecs, wmain, wsm2, w2, w_out[i].astype(BF16))
    return x
```

```python
import jax
import jax.numpy as jnp
from jax import lax
from jax.experimental import pallas as pl
from jax.experimental.pallas import tpu as pltpu

F32 = jnp.float32
BF16 = jnp.bfloat16

NORM_EPS = 1e-6
D_MODEL = 1024
GLA_HEADS = 4
GLA_DK = 128
GLA_DV = 256
GLA_KEY = GLA_HEADS * GLA_DK
GLA_VAL = GLA_HEADS * GLA_DV
GLA_GATE_RANK = 16
GLA_TAU = 16.0
GLA_CHUNK = 64
SSD_INNER = 1024
SSD_HEAD_DIM = 64
SSD_HEADS = 16
SSD_GROUPS = 2
SSD_HEADS_PER_GROUP = SSD_HEADS // SSD_GROUPS
SSD_GROUP_WIDTH = SSD_INNER // SSD_GROUPS
SSD_STATE = 128
SSD_CONV = 4
SSD_CHUNK = 128
SSD_CONV_DIM = SSD_INNER + 2 * SSD_GROUPS * SSD_STATE
D_MIX = GLA_VAL + SSD_INNER

LANES = 128
SUBLANES = 8

TIME_BLOCK = 512
ROW_CHUNK = 128
PROJ_TILE = 512
SUB_BLOCK = 256
COL_QK = 0
COL_V = COL_QK + 2 * GLA_KEY
COL_G = COL_V + GLA_VAL
COL_Z = COL_G + GLA_VAL
COL_XBC = COL_Z + SSD_INNER
W_MAIN_COLS = COL_XBC + SSD_CONV_DIM
DT_LANE = GLA_GATE_RANK
assert DT_LANE + SSD_HEADS <= LANES
IN_GATE = 2 * GLA_KEY + 2 * GLA_VAL
IN_Z = IN_GATE + GLA_GATE_RANK
IN_DT = IN_Z + SSD_INNER + SSD_CONV_DIM
D_PROJ = IN_DT + SSD_HEADS
assert IN_GATE % PROJ_TILE == 0 and W_MAIN_COLS % PROJ_TILE == 0 and SSD_INNER % PROJ_TILE == 0
assert 2 * SSD_HEAD_DIM == LANES
MASKED_LOG = -1e30
(VEC_PRE_W, VEC_GATE_B, VEC_GLA_NORM, VEC_CONV_B, VEC_CONV_W, VEC_DT_BIAS, VEC_A_LOG,
 VEC_D_SKIP, VEC_SSD_NORM, VEC_POST_W) = (0, 1, 2, 3, 4, 4 + SSD_CONV, 5 + SSD_CONV, 6 + SSD_CONV,
                                          7 + SSD_CONV, 8 + SSD_CONV)
VEC_USED = VEC_POST_W + 1
VEC_ROWS = -(-VEC_USED // SUBLANES) * SUBLANES
VEC_COLS = SSD_CONV_DIM
CONV_PAD = SUBLANES
VMEM_LIMIT_BYTES = 56 * 1024 * 1024


def _dot(a, b):
    return jnp.dot(a, b, preferred_element_type=F32)


def _dot_nt(a, b):
    return lax.dot_general(a, b, (((1,), (1,)), ((), ())), preferred_element_type=F32)


def _dot_tn(a, b):
    return lax.dot_general(a, b, (((0,), (0,)), ((), ())), preferred_element_type=F32)


def _split(x):
    hi = x.astype(BF16)
    lo = (x - hi.astype(F32)).astype(BF16)
    return hi, lo


def _silu(x):
    hx = 0.5 * x
    return hx + hx * jnp.tanh(hx)


def _log1p_exp_neg_abs(x):
    return jnp.log(1.0 + jnp.exp(-jnp.abs(x)))


def _tril(n, dtype):
    r = lax.broadcasted_iota(jnp.int32, (n, n), 0)
    c = lax.broadcasted_iota(jnp.int32, (n, n), 1)
    return r >= c, (r >= c).astype(dtype)


def _ada_kernel(c_ref, w_ref, b_ref, o_ref):
    c = c_ref[...]
    ca_hi, ca_lo = _split(_silu(c))
    w_hi, w_lo = _split(w_ref[...])
    o_ref[...] = _dot(ca_hi, w_hi) + _dot(ca_hi, w_lo) + _dot(ca_lo, w_hi) + b_ref[...]


def _ada(c_pad, ada_w, ada_b):
    rows = c_pad.shape[0]
    return pl.pallas_call(
        _ada_kernel,
        grid=(3,),
        in_specs=[
            pl.BlockSpec((rows, D_MODEL), lambda j: (0, 0)),
            pl.BlockSpec((D_MODEL, D_MODEL), lambda j: (0, j)),
            pl.BlockSpec((1, D_MODEL), lambda j: (0, j)),
        ],
        out_specs=pl.BlockSpec((rows, D_MODEL), lambda j: (0, j)),
        out_shape=jax.ShapeDtypeStruct((rows, 3 * D_MODEL), F32),
        compiler_params=pltpu.CompilerParams(dimension_semantics=("arbitrary",)),
        name="adaln",
    )(c_pad, ada_w, ada_b)


def _layer_kernel(x_ref, ada_ref, vec_ref, wmain_ref, wsm_ref, w2_ref, wout_ref,
                  out_ref,
                  h_s, sm_s, xbc_s, u_s, b_s, qd_s, kd_s, ke_s, dec_s, v_s, gact_s, zact_s, mix_s,
                  gstate_s, sstate_s, w2a_s, w2b_s, expand_s, xlo_s, xhi_s, bc_s,
                  gstate_b_s, sstate_b_s, att_s, w_s, xss_s, decin_s, cdec_s):
    tb = TIME_BLOCK
    t = pl.program_id(1)

    @pl.when(t == 0)
    def _():
        gstate_s[...] = jnp.zeros_like(gstate_s)
        sstate_s[...] = jnp.zeros_like(sstate_s)
        gstate_b_s[...] = jnp.zeros_like(gstate_b_s)
        sstate_b_s[...] = jnp.zeros_like(sstate_b_s)
        xbc_s[0:CONV_PAD, :] = jnp.zeros((CONV_PAD, SSD_CONV_DIM), F32)
        w2_hi, w2_lo = _split(w2_ref[...])
        w2a_s[0:LANES, :] = w2_hi
        w2a_s[LANES:2 * LANES, :] = w2_hi
        w2b_s[...] = w2_lo
        head_row = lax.broadcasted_iota(jnp.int32, (2 * LANES, SSD_INNER), 0) % LANES
        head_of_col = lax.broadcasted_iota(jnp.int32, (2 * LANES, SSD_INNER), 1) // SSD_HEAD_DIM + DT_LANE
        expand_s[...] = (head_row == head_of_col).astype(BF16)

    def vec_view(row, width, rows=1):
        return vec_ref.at[row:row + rows, 0:width]

    prew_ref = vec_view(VEC_PRE_W, D_MODEL)
    gb_ref = vec_view(VEC_GATE_B, GLA_KEY)
    gnw_ref = vec_view(VEC_GLA_NORM, GLA_DV)
    cb_ref = vec_view(VEC_CONV_B, SSD_CONV_DIM)
    cw_ref = vec_view(VEC_CONV_W, SSD_CONV_DIM, SSD_CONV)
    dtb_ref = vec_view(VEC_DT_BIAS, LANES)
    alog_ref = vec_view(VEC_A_LOG, LANES)
    dskip_ref = vec_view(VEC_D_SKIP, SSD_INNER)
    snw_ref = vec_view(VEC_SSD_NORM, SSD_INNER)
    postw_ref = vec_view(VEC_POST_W, D_MODEL)

    ada = ada_ref[pl.ds(pl.program_id(0), 1), :]
    shift = ada[:, 0:D_MODEL]
    scale1 = 1.0 + ada[:, D_MODEL:2 * D_MODEL]
    gate = ada[:, 2 * D_MODEL:3 * D_MODEL]
    causal_g, tril_g = _tril(GLA_CHUNK, BF16)
    tril_g2 = jnp.concatenate([tril_g, tril_g], axis=1)

    def project_tasks(r0):
        sub = slice(r0, r0 + SUB_BLOCK)
        row_chunks = [slice(r, r + ROW_CHUNK) for r in range(r0, r0 + SUB_BLOCK, ROW_CHUNK)]
        tasks = []

        def proj(col, width):
            return _dot(h_s[sub, :], wmain_ref[:, col:col + width])

        def prenorm(rs):
            x = x_ref[0, rs, :]
            ms = jnp.mean(x * x, axis=-1, keepdims=True)
            h = (x * lax.rsqrt(ms + NORM_EPS)) * prew_ref[...] * scale1 + shift
            h_hi = h.astype(BF16)
            h_lo = (h - h_hi.astype(F32)).astype(BF16)
            h_s[rs, :] = h_hi
            both = _dot(h_hi, wsm_ref[...])
            sm_s[rs, :] = both[:, 0:LANES] + both[:, LANES:2 * LANES] + _dot(h_lo, wsm_ref[:, 0:LANES])

        def forget_gate(rs):
            sm_hi, sm_lo = _split(sm_s[rs, :])
            sm_both = jnp.concatenate([sm_hi, sm_lo], axis=1)
            gl = _dot(sm_both, w2a_s[...]) + _dot(sm_hi, w2b_s[...]) + gb_ref[...]
            la = (jnp.minimum(gl, 0.0) - _log1p_exp_neg_abs(gl)) * (1.0 / GLA_TAU)
            for cc in range(ROW_CHUNK // GLA_CHUNK):
                la_hi, la_lo = _split(la[cc * GLA_CHUNK:(cc + 1) * GLA_CHUNK, :])
                b = _dot(tril_g2, jnp.concatenate([la_hi, la_lo], axis=0))
                c0 = rs.start + cc * GLA_CHUNK
                b_s[c0:c0 + GLA_CHUNK, :] = b
                ci = c0 // GLA_CHUNK
                dec_s[ci:ci + 1, :] = jnp.exp(b[GLA_CHUNK - 1:GLA_CHUNK, :])

        def xbc_tile(cs):
            xbc_s[CONV_PAD + r0:CONV_PAD + r0 + SUB_BLOCK, cs] = proj(COL_XBC + cs.start, PROJ_TILE)

        def conv(rs, cs):
            a = xbc_s[rs.start:rs.start + ROW_CHUNK + CONV_PAD, cs]
            acc = cw_ref[0:1, cs] * a
            for kk in range(1, SSD_CONV):
                acc = pltpu.roll(acc, 1, 0) + cw_ref[kk:kk + 1, cs] * a
            u = _silu(acc[CONV_PAD:, :] + cb_ref[:, cs])
            if cs.start < SSD_INNER:
                u_s[rs, cs] = u
                first_head = lax.broadcasted_iota(jnp.int32, u.shape, 1) % LANES < SSD_HEAD_DIM
                xlo_s[rs, cs] = jnp.where(first_head, u, 0.0).astype(BF16)
                xhi_s[rs, cs] = jnp.where(first_head, 0.0, u).astype(BF16)
            else:
                bc_s[rs, cs.start - SSD_INNER:cs.stop - SSD_INNER] = u.astype(BF16)

        def q_tile():
            q = proj(COL_QK, GLA_KEY)
            qd_s[sub, :] = (q * (GLA_DK ** -0.5) * jnp.exp(b_s[sub, :])).astype(BF16)

        def k_tile():
            k = proj(COL_QK + GLA_KEY, GLA_KEY)
            for cc in range(SUB_BLOCK // GLA_CHUNK):
                rows = slice(r0 + cc * GLA_CHUNK, r0 + (cc + 1) * GLA_CHUNK)
                b = b_s[rows, :]
                k_c = k[cc * GLA_CHUNK:(cc + 1) * GLA_CHUNK, :]
                kd_s[rows, :] = (k_c * jnp.exp(-b)).astype(BF16)
                ke_s[rows, :] = (k_c * jnp.exp(b[GLA_CHUNK - 1:GLA_CHUNK, :] - b)).astype(BF16)

        def act_tile(dst_s, col, cs):
            dst_s[sub, cs] = _silu(proj(col + cs.start, PROJ_TILE).astype(BF16))

        def v_tile(cs):
            v_s[sub, cs] = proj(COL_V + cs.start, PROJ_TILE).astype(BF16)

        for rs in row_chunks:
            tasks.append(lambda rs=rs: prenorm(rs))
        for rs in row_chunks:
            tasks.append(lambda rs=rs: forget_gate(rs))
        for col in range(0, SSD_CONV_DIM, PROJ_TILE):
            cs = slice(col, col + PROJ_TILE)
            tasks.append(lambda cs=cs: xbc_tile(cs))
            for rs in row_chunks:
                tasks.append(lambda rs=rs, cs=cs: conv(rs, cs))
        ssd_chunks = range(r0 // SSD_CHUNK, (r0 + SUB_BLOCK) // SSD_CHUNK)
        ssd_shared = {ci: {} for ci in ssd_chunks}
        for ci in ssd_chunks:
            tasks.append(lambda ci=ci: ssd_decays(ci, ssd_shared[ci]))
        tasks.append(q_tile)
        tasks.append(k_tile)
        for ci in ssd_chunks:
            for g in range(SSD_GROUPS):
                tasks.append(lambda ci=ci, g=g: ssd_weights(ci, g, ssd_shared[ci]))
        for ci in range(r0 // GLA_CHUNK, (r0 + SUB_BLOCK) // GLA_CHUNK):
            tasks.append(lambda ci=ci: gla_scores(ci))
        for col in range(0, GLA_VAL, PROJ_TILE):
            cs = slice(col, col + PROJ_TILE)
            tasks.append(lambda cs=cs: act_tile(gact_s, COL_G, cs))
        for col in range(0, SSD_INNER, PROJ_TILE):
            cs = slice(col, col + PROJ_TILE)
            tasks.append(lambda cs=cs: act_tile(zact_s, COL_Z, cs))
        for col in range(0, GLA_VAL, PROJ_TILE):
            cs = slice(col, col + PROJ_TILE)
            tasks.append(lambda cs=cs: v_tile(cs))
        return tasks

    def gla_scores(ci):
        rows = slice(ci * GLA_CHUNK, (ci + 1) * GLA_CHUNK)
        for hh in range(GLA_HEADS):
            ks = slice(hh * GLA_DK, (hh + 1) * GLA_DK)
            att = _dot_nt(qd_s[rows, ks], kd_s[rows, ks])
            att_s[hh, rows, :] = jnp.where(causal_g, att, 0.0).astype(BF16)

    def gla_chunk(ci):
        rows = slice(ci * GLA_CHUNK, (ci + 1) * GLA_CHUNK)
        decay = dec_s[ci:ci + 1, :]
        for hh in range(GLA_HEADS):
            ks = slice(hh * GLA_DK, (hh + 1) * GLA_DK)
            vs = slice(hh * GLA_DV, (hh + 1) * GLA_DV)
            v_h = v_s[rows, vs]
            o = _dot(att_s[hh, rows, :], v_h) + _dot_nt(qd_s[rows, ks], gstate_b_s[hh])
            state_t = gstate_s[hh] * decay[:, ks] + _dot_tn(v_h, ke_s[rows, ks])
            gstate_s[hh] = state_t
            gstate_b_s[hh] = state_t.astype(BF16)
            o = (o * lax.rsqrt(jnp.mean(o * o, axis=-1, keepdims=True) + NORM_EPS)) * gnw_ref[...]
            mix_s[rows, vs] = (o * gact_s[rows, vs].astype(F32)).astype(BF16)

    causal_s, tril_s = _tril(SSD_CHUNK, BF16)
    tril_s2 = jnp.concatenate([tril_s, tril_s], axis=1)
    a_neg = -jnp.exp(alog_ref[...])

    def expand_heads(v):
        hi, lo = _split(v)
        return _dot(jnp.concatenate([hi, lo], axis=1), expand_s[...])

    def ssd_decays(ci, shared):
        rows = slice(ci * SSD_CHUNK, (ci + 1) * SSD_CHUNK)
        dt_in = sm_s[rows, :] + dtb_ref[...]
        dt = jnp.maximum(dt_in, 0.0) + _log1p_exp_neg_abs(dt_in)
        dta_hi, dta_lo = _split(dt * a_neg)
        a_cum = _dot(tril_s2, jnp.concatenate([dta_hi, dta_lo], axis=0))
        a_last = a_cum[SSD_CHUNK - 1:SSD_CHUNK, :]
        decay_in_x = expand_heads(jnp.exp(a_cum))
        decin_s[rows, :] = decay_in_x
        cdec_s[ci:ci + 1, :] = decay_in_x[SSD_CHUNK - 1:SSD_CHUNK, :]
        xss_s[rows, :] = (u_s[rows, :] * expand_heads(jnp.exp(a_last - a_cum) * dt)).astype(BF16)
        shared.update(a_cum=a_cum, src_t=(a_cum - jnp.log(dt)).T)

    def ssd_weights(ci, g, shared):
        rows = slice(ci * SSD_CHUNK, (ci + 1) * SSD_CHUNK)
        b_g = bc_s[rows, g * SSD_STATE:(g + 1) * SSD_STATE]
        c_g = bc_s[rows, (SSD_GROUPS + g) * SSD_STATE:(SSD_GROUPS + g + 1) * SSD_STATE]
        a_cum, src_t = shared["a_cum"], shared["src_t"]
        cb = _dot_nt(c_g, b_g)
        for hl in range(SSD_HEADS_PER_GROUP):
            head = g * SSD_HEADS_PER_GROUP + hl
            lane = DT_LANE + head
            seg = jnp.where(causal_s, a_cum[:, lane:lane + 1] - src_t[lane:lane + 1, :], MASKED_LOG)
            w_s[rows, head * SSD_CHUNK:(head + 1) * SSD_CHUNK] = (jnp.exp(seg) * cb).astype(BF16)

    def ssd_group(ci, g):
        rows = slice(ci * SSD_CHUNK, (ci + 1) * SSD_CHUNK)
        gs = slice(g * SSD_GROUP_WIDTH, (g + 1) * SSD_GROUP_WIDTH)
        b_g = bc_s[rows, g * SSD_STATE:(g + 1) * SSD_STATE]
        c_g = bc_s[rows, (SSD_GROUPS + g) * SSD_STATE:(SSD_GROUPS + g + 1) * SSD_STATE]
        y_g = _dot(c_g, sstate_b_s[g]) * decin_s[rows, gs]
        state = sstate_s[g] * cdec_s[ci:ci + 1, gs] + _dot_tn(b_g, xss_s[rows, gs])
        sstate_s[g] = state
        sstate_b_s[g] = state.astype(BF16)
        y_tiles = []
        for pair in range(SSD_HEADS_PER_GROUP // 2):
            ts = slice(gs.start + pair * LANES, gs.start + (pair + 1) * LANES)
            head = g * SSD_HEADS_PER_GROUP + 2 * pair
            x_pair = jnp.concatenate([xlo_s[rows, ts], xhi_s[rows, ts]], axis=0)
            y_tiles.append(_dot(w_s[rows, head * SSD_CHUNK:(head + 2) * SSD_CHUNK], x_pair))
        y_g = y_g + jnp.concatenate(y_tiles, axis=1) + dskip_ref[:, gs] * u_s[rows, gs]
        y_g = y_g * zact_s[rows, gs].astype(F32)
        y_g = (y_g * lax.rsqrt(jnp.mean(y_g * y_g, axis=-1, keepdims=True) + NORM_EPS)) * snw_ref[:, gs]
        mix_s[rows, GLA_VAL + g * SSD_GROUP_WIDTH:GLA_VAL + (g + 1) * SSD_GROUP_WIDTH] = y_g.astype(BF16)

    def out_rows(rs):
        mixed = _dot(mix_s[rs, :], wout_ref[...])
        mixed = (mixed * lax.rsqrt(jnp.mean(mixed * mixed, axis=-1, keepdims=True) + NORM_EPS)) * postw_ref[...]
        out_ref[0, rs, :] = x_ref[0, rs, :] + gate * mixed

    def recur_tasks(r0):
        tasks = []
        gla_per_ssd = SSD_CHUNK // GLA_CHUNK
        for ci in range(r0 // SSD_CHUNK, (r0 + SUB_BLOCK) // SSD_CHUNK):
            for cc in range(gla_per_ssd):
                tasks.append(lambda ci=ci, cc=cc: gla_chunk(ci * gla_per_ssd + cc))
            for g in range(SSD_GROUPS):
                tasks.append(lambda ci=ci, g=g: ssd_group(ci, g))
        tasks.append(lambda: out_rows(slice(r0, r0 + SUB_BLOCK)))
        return tasks

    def issue_alternating(a, b):
        order = sorted([((i + 0.5) / len(a), 0, i) for i in range(len(a))]
                       + [((i + 0.5) / len(b), 1, i) for i in range(len(b))])
        for _, which, i in order:
            (a, b)[which][i]()

    sub_starts = list(range(0, tb, SUB_BLOCK))
    for task in project_tasks(sub_starts[0]):
        task()
    for prev, cur in zip(sub_starts[:-1], sub_starts[1:]):
        issue_alternating(project_tasks(cur), recur_tasks(prev))
    for task in recur_tasks(sub_starts[-1]):
        task()

    xbc_s[0:CONV_PAD, :] = xbc_s[tb:tb + CONV_PAD, :]


def _const_spec(shape):
    zeros = (0,) * len(shape)
    return pl.BlockSpec(shape, lambda b, t: zeros, pipeline_mode=pl.Buffered(1))


def _layer(x, ada, vecs, wmain, wsm, w2, wout):
    bsz, seq, _ = x.shape
    tb = TIME_BLOCK
    consts = (ada, vecs, wmain, wsm, w2, wout)
    return pl.pallas_call(
        _layer_kernel,
        grid=(bsz, seq // tb),
        in_specs=[pl.BlockSpec((1, tb, D_MODEL), lambda b, t: (b, t, 0))]
        + [_const_spec(a.shape) for a in consts],
        out_specs=pl.BlockSpec((1, tb, D_MODEL), lambda b, t: (b, t, 0)),
        out_shape=jax.ShapeDtypeStruct(x.shape, x.dtype),
        scratch_shapes=[
            pltpu.VMEM((tb, D_MODEL), BF16),
            pltpu.VMEM((tb, LANES), F32),
            pltpu.VMEM((tb + CONV_PAD, SSD_CONV_DIM), F32),
            pltpu.VMEM((tb, SSD_INNER), F32),
            pltpu.VMEM((tb, GLA_KEY), F32),
            pltpu.VMEM((tb, GLA_KEY), BF16),
            pltpu.VMEM((tb, GLA_KEY), BF16),
            pltpu.VMEM((tb, GLA_KEY), BF16),
            pltpu.VMEM((tb // GLA_CHUNK, GLA_KEY), F32),
            pltpu.VMEM((tb, GLA_VAL), BF16),
            pltpu.VMEM((tb, GLA_VAL), BF16),
            pltpu.VMEM((tb, SSD_INNER), BF16),
            pltpu.VMEM((tb, D_MIX), BF16),
            pltpu.VMEM((GLA_HEADS, GLA_DV, GLA_DK), F32),
            pltpu.VMEM((SSD_GROUPS, SSD_STATE, SSD_GROUP_WIDTH), F32),
            pltpu.VMEM((2 * LANES, GLA_KEY), BF16),
            pltpu.VMEM((LANES, GLA_KEY), BF16),
            pltpu.VMEM((2 * LANES, SSD_INNER), BF16),
            pltpu.VMEM((tb, SSD_INNER), BF16),
            pltpu.VMEM((tb, SSD_INNER), BF16),
            pltpu.VMEM((tb, 2 * SSD_GROUPS * SSD_STATE), BF16),
            pltpu.VMEM((GLA_HEADS, GLA_DV, GLA_DK), BF16),
            pltpu.VMEM((SSD_GROUPS, SSD_STATE, SSD_GROUP_WIDTH), BF16),
            pltpu.VMEM((GLA_HEADS, tb, GLA_CHUNK), BF16),
            pltpu.VMEM((tb, SSD_HEADS * SSD_CHUNK), BF16),
            pltpu.VMEM((tb, SSD_INNER), BF16),
            pltpu.VMEM((tb, SSD_INNER), F32),
            pltpu.VMEM((tb // SSD_CHUNK, SSD_INNER), F32),
        ],
        compiler_params=pltpu.CompilerParams(
            dimension_semantics=("arbitrary", "arbitrary"),
            vmem_limit_bytes=VMEM_LIMIT_BYTES),
        name="hybrid_layer",
    )(x, *consts)


def _prep_kernel(wt_ref, gate_ref, dt_ref, wmain_ref, wsm_ref):
    wmain_ref[...] = wt_ref[...].T.astype(BF16)

    @pl.when(pl.program_id(0) == 0)
    def _():
        narrow_t = jnp.concatenate(
            [gate_ref[...], dt_ref[...], jnp.zeros((LANES - DT_LANE - SSD_HEADS, D_MODEL), F32)], axis=0)
        hi, lo = _split(narrow_t.T)
        wsm_ref[:, 0:LANES] = hi
        wsm_ref[:, LANES:2 * LANES] = lo


def _prep_weights(w_t):
    def src_row(j):
        return pl.multiple_of(j * PROJ_TILE + jnp.where(j * PROJ_TILE >= IN_GATE, GLA_GATE_RANK, 0), SUBLANES)

    return pl.pallas_call(
        _prep_kernel,
        grid=(W_MAIN_COLS // PROJ_TILE,),
        in_specs=[pl.BlockSpec((pl.Element(PROJ_TILE), pl.Element(D_MODEL)), lambda j: (src_row(j), 0)),
                  pl.BlockSpec((pl.Element(GLA_GATE_RANK), pl.Element(D_MODEL)), lambda j: (IN_GATE, 0)),
                  pl.BlockSpec((pl.Element(SSD_HEADS), pl.Element(D_MODEL)), lambda j: (IN_DT, 0))],
        out_specs=[pl.BlockSpec((D_MODEL, PROJ_TILE), lambda j: (0, j)),
                   pl.BlockSpec((D_MODEL, 2 * LANES), lambda j: (0, 0))],
        out_shape=[jax.ShapeDtypeStruct((D_MODEL, W_MAIN_COLS), BF16),
                   jax.ShapeDtypeStruct((D_MODEL, 2 * LANES), BF16)],
        compiler_params=pltpu.CompilerParams(dimension_semantics=("arbitrary",)),
        name="prep_weights",
    )(w_t, w_t, w_t)


def _row(v, offset=0):
    tail = VEC_COLS - offset - v.shape[0]
    return [jnp.zeros((offset,), F32), v.astype(F32), jnp.zeros((tail,), F32)]


def kernel(x, c, ada_w, ada_b, pre_norm_w, w_in, gla_gate_w2, gla_gate_b, gla_norm_w, conv_w, conv_b,
           dt_bias, a_log, d_skip, ssd_norm_w, w_out, post_norm_w):
    bsz, seq, _ = x.shape
    assert seq % TIME_BLOCK == 0 and bsz <= SUBLANES
    depth = ada_w.shape[0]
    c_pad = jnp.pad(c, ((0, SUBLANES - bsz), (0, 0)))
    for i in range(depth):
        ada = _ada(c_pad, ada_w[i], ada_b[i][None, :])
        wmain, wsm2 = _prep_weights(w_in[i].T)
        rows = ([_row(pre_norm_w[i]), _row(gla_gate_b[i]), _row(gla_norm_w[i]), _row(conv_b[i])]
                + [_row(conv_w[i, kk]) for kk in range(SSD_CONV)]
                + [_row(dt_bias[i], DT_LANE), _row(a_log[i], DT_LANE),
                   _row(jnp.repeat(d_skip[i], SSD_HEAD_DIM)), _row(ssd_norm_w[i]), _row(post_norm_w[i])]
                + [[jnp.zeros(((VEC_ROWS - VEC_USED) * VEC_COLS,), F32)]])
        vecs = jnp.concatenate([piece for row in rows for piece in row]).reshape(VEC_ROWS, VEC_COLS)
        w2 = jnp.pad(gla_gate_w2[i], ((0, LANES - GLA_GATE_RANK), (0, 0)))
        x = _layer(x, ada, vecs, wmain, wsm2, w2, w_out[i].astype(BF16))
    return x
```

```python
import jax
import jax.numpy as jnp
from jax import lax
from jax.experimental import pallas as pl
from jax.experimental.pallas import tpu as pltpu

F32 = jnp.float32
BF16 = jnp.bfloat16

NORM_EPS = 1e-6
D_MODEL = 1024
GLA_HEADS = 4
GLA_DK = 128
GLA_DV = 256
GLA_KEY = GLA_HEADS * GLA_DK
GLA_VAL = GLA_HEADS * GLA_DV
GLA_GATE_RANK = 16
GLA_TAU = 16.0
GLA_CHUNK = 64
SSD_INNER = 1024
SSD_HEAD_DIM = 64
SSD_HEADS = 16
SSD_GROUPS = 2
SSD_HEADS_PER_GROUP = SSD_HEADS // SSD_GROUPS
SSD_GROUP_WIDTH = SSD_INNER // SSD_GROUPS
SSD_STATE = 128
SSD_CONV = 4
SSD_CHUNK = 128
SSD_CONV_DIM = SSD_INNER + 2 * SSD_GROUPS * SSD_STATE
D_MIX = GLA_VAL + SSD_INNER

LANES = 128
SUBLANES = 8

TIME_BLOCK = 1024
ROW_CHUNK = 128
PROJ_TILE = 512
SUB_BLOCK = 256
COL_QK = 0
COL_V = COL_QK + 2 * GLA_KEY
COL_G = COL_V + GLA_VAL
COL_Z = COL_G + GLA_VAL
COL_XBC = COL_Z + SSD_INNER
W_MAIN_COLS = COL_XBC + SSD_CONV_DIM
DT_LANE = GLA_GATE_RANK
assert DT_LANE + SSD_HEADS <= LANES
IN_GATE = 2 * GLA_KEY + 2 * GLA_VAL
IN_Z = IN_GATE + GLA_GATE_RANK
IN_DT = IN_Z + SSD_INNER + SSD_CONV_DIM
D_PROJ = IN_DT + SSD_HEADS
assert IN_GATE % PROJ_TILE == 0 and W_MAIN_COLS % PROJ_TILE == 0 and SSD_INNER % PROJ_TILE == 0
assert 2 * SSD_HEAD_DIM == LANES
MASKED_LOG = -1e30
(VEC_PRE_W, VEC_GATE_B, VEC_GLA_NORM, VEC_CONV_B, VEC_CONV_W, VEC_DT_BIAS, VEC_A_LOG,
 VEC_D_SKIP, VEC_SSD_NORM, VEC_POST_W) = (0, 1, 2, 3, 4, 4 + SSD_CONV, 5 + SSD_CONV, 6 + SSD_CONV,
                                          7 + SSD_CONV, 8 + SSD_CONV)
VEC_USED = VEC_POST_W + 1
VEC_ROWS = -(-VEC_USED // SUBLANES) * SUBLANES
VEC_COLS = SSD_CONV_DIM
CONV_PAD = SUBLANES
VMEM_LIMIT_BYTES = 62 * 1024 * 1024


def _dot(a, b):
    return jnp.dot(a, b, preferred_element_type=F32)


def _dot_nt(a, b):
    return lax.dot_general(a, b, (((1,), (1,)), ((), ())), preferred_element_type=F32)


def _dot_tn(a, b):
    return lax.dot_general(a, b, (((0,), (0,)), ((), ())), preferred_element_type=F32)


def _split(x):
    hi = x.astype(BF16)
    lo = (x - hi.astype(F32)).astype(BF16)
    return hi, lo


def _silu(x):
    hx = 0.5 * x
    return hx + hx * jnp.tanh(hx)


def _log1p_exp_neg_abs(x):
    return jnp.log(1.0 + jnp.exp(-jnp.abs(x)))


def _tril(n, dtype):
    r = lax.broadcasted_iota(jnp.int32, (n, n), 0)
    c = lax.broadcasted_iota(jnp.int32, (n, n), 1)
    return r >= c, (r >= c).astype(dtype)


def _ada_kernel(c_ref, w_ref, b_ref, o_ref):
    c = c_ref[...]
    ca_hi, ca_lo = _split(_silu(c))
    w_hi, w_lo = _split(w_ref[...])
    o_ref[...] = _dot(ca_hi, w_hi) + _dot(ca_hi, w_lo) + _dot(ca_lo, w_hi) + b_ref[...]


def _ada(c_pad, ada_w, ada_b):
    rows = c_pad.shape[0]
    return pl.pallas_call(
        _ada_kernel,
        grid=(3,),
        in_specs=[
            pl.BlockSpec((rows, D_MODEL), lambda j: (0, 0)),
            pl.BlockSpec((D_MODEL, D_MODEL), lambda j: (0, j)),
            pl.BlockSpec((1, D_MODEL), lambda j: (0, j)),
        ],
        out_specs=pl.BlockSpec((rows, D_MODEL), lambda j: (0, j)),
        out_shape=jax.ShapeDtypeStruct((rows, 3 * D_MODEL), F32),
        compiler_params=pltpu.CompilerParams(dimension_semantics=("arbitrary",)),
        name="adaln",
    )(c_pad, ada_w, ada_b)


def _layer_kernel(x_ref, ada_ref, vec_ref, wmain_ref, wsm_ref, w2_ref, wout_ref,
                  out_ref,
                  h_s, sm_s, xbc_s, u_s, b_s, qd_s, kd_s, ke_s, dec_s, v_s, gact_s, zact_s, mix_s,
                  gstate_s, sstate_s, w2a_s, w2b_s, expand_s, xlo_s, xhi_s, bc_s,
                  gstate_b_s, sstate_b_s, att_s, w_s, xss_s, decin_s, cdec_s):
    tb = TIME_BLOCK
    t = pl.program_id(1)

    @pl.when(t == 0)
    def _():
        gstate_s[...] = jnp.zeros_like(gstate_s)
        sstate_s[...] = jnp.zeros_like(sstate_s)
        gstate_b_s[...] = jnp.zeros_like(gstate_b_s)
        sstate_b_s[...] = jnp.zeros_like(sstate_b_s)
        xbc_s[0:CONV_PAD, :] = jnp.zeros((CONV_PAD, SSD_CONV_DIM), F32)
        w2_hi, w2_lo = _split(w2_ref[...])
        w2a_s[0:LANES, :] = w2_hi
        w2a_s[LANES:2 * LANES, :] = w2_hi
        w2b_s[...] = w2_lo
        head_row = lax.broadcasted_iota(jnp.int32, (2 * LANES, SSD_INNER), 0) % LANES
        head_of_col = lax.broadcasted_iota(jnp.int32, (2 * LANES, SSD_INNER), 1) // SSD_HEAD_DIM + DT_LANE
        expand_s[...] = (head_row == head_of_col).astype(BF16)

    def vec_view(row, width, rows=1):
        return vec_ref.at[row:row + rows, 0:width]

    prew_ref = vec_view(VEC_PRE_W, D_MODEL)
    gb_ref = vec_view(VEC_GATE_B, GLA_KEY)
    gnw_ref = vec_view(VEC_GLA_NORM, GLA_DV)
    cb_ref = vec_view(VEC_CONV_B, SSD_CONV_DIM)
    cw_ref = vec_view(VEC_CONV_W, SSD_CONV_DIM, SSD_CONV)
    dtb_ref = vec_view(VEC_DT_BIAS, LANES)
    alog_ref = vec_view(VEC_A_LOG, LANES)
    dskip_ref = vec_view(VEC_D_SKIP, SSD_INNER)
    snw_ref = vec_view(VEC_SSD_NORM, SSD_INNER)
    postw_ref = vec_view(VEC_POST_W, D_MODEL)

    ada = ada_ref[pl.ds(pl.program_id(0), 1), :]
    shift = ada[:, 0:D_MODEL]
    scale1 = 1.0 + ada[:, D_MODEL:2 * D_MODEL]
    gate = ada[:, 2 * D_MODEL:3 * D_MODEL]
    causal_g, tril_g = _tril(GLA_CHUNK, BF16)
    tril_g2 = jnp.concatenate([tril_g, tril_g], axis=1)

    def project_tasks(r0):
        sub = slice(r0, r0 + SUB_BLOCK)
        row_chunks = [slice(r, r + ROW_CHUNK) for r in range(r0, r0 + SUB_BLOCK, ROW_CHUNK)]
        tasks = []

        def proj(col, width):
            return _dot(h_s[sub, :], wmain_ref[:, col:col + width])

        def prenorm(rs):
            x = x_ref[0, rs, :]
            ms = jnp.mean(x * x, axis=-1, keepdims=True)
            h = (x * lax.rsqrt(ms + NORM_EPS)) * prew_ref[...] * scale1 + shift
            h_hi = h.astype(BF16)
            h_lo = (h - h_hi.astype(F32)).astype(BF16)
            h_s[rs, :] = h_hi
            both = _dot(h_hi, wsm_ref[...])
            sm_s[rs, :] = both[:, 0:LANES] + both[:, LANES:2 * LANES] + _dot(h_lo, wsm_ref[:, 0:LANES])

        def forget_gate(rs):
            sm_hi, sm_lo = _split(sm_s[rs, :])
            sm_both = jnp.concatenate([sm_hi, sm_lo], axis=1)
            gl = _dot(sm_both, w2a_s[...]) + _dot(sm_hi, w2b_s[...]) + gb_ref[...]
            la = (jnp.minimum(gl, 0.0) - _log1p_exp_neg_abs(gl)) * (1.0 / GLA_TAU)
            for cc in range(ROW_CHUNK // GLA_CHUNK):
                la_hi, la_lo = _split(la[cc * GLA_CHUNK:(cc + 1) * GLA_CHUNK, :])
                b = _dot(tril_g2, jnp.concatenate([la_hi, la_lo], axis=0))
                c0 = rs.start + cc * GLA_CHUNK
                b_s[c0:c0 + GLA_CHUNK, :] = b
                ci = c0 // GLA_CHUNK
                dec_s[ci:ci + 1, :] = jnp.exp(b[GLA_CHUNK - 1:GLA_CHUNK, :])

        def xbc_tile(cs):
            xbc_s[CONV_PAD:CONV_PAD + SUB_BLOCK, cs] = proj(COL_XBC + cs.start, PROJ_TILE)

        def conv(rs, cs):
            a = xbc_s[rs.start - r0:rs.start - r0 + ROW_CHUNK + CONV_PAD, cs]
            acc = cw_ref[0:1, cs] * a
            for kk in range(1, SSD_CONV):
                acc = pltpu.roll(acc, 1, 0) + cw_ref[kk:kk + 1, cs] * a
            u = _silu(acc[CONV_PAD:, :] + cb_ref[:, cs])
            if cs.start < SSD_INNER:
                u_s[rs, cs] = u
                first_head = lax.broadcasted_iota(jnp.int32, u.shape, 1) % LANES < SSD_HEAD_DIM
                xlo_s[rs, cs] = jnp.where(first_head, u, 0.0).astype(BF16)
                xhi_s[rs, cs] = jnp.where(first_head, 0.0, u).astype(BF16)
            else:
                bc_s[rs, cs.start - SSD_INNER:cs.stop - SSD_INNER] = u.astype(BF16)

        def q_tile():
            q = proj(COL_QK, GLA_KEY)
            qd_s[sub, :] = (q * (GLA_DK ** -0.5) * jnp.exp(b_s[sub, :])).astype(BF16)

        def k_tile():
            k = proj(COL_QK + GLA_KEY, GLA_KEY)
            for cc in range(SUB_BLOCK // GLA_CHUNK):
                rows = slice(r0 + cc * GLA_CHUNK, r0 + (cc + 1) * GLA_CHUNK)
                b = b_s[rows, :]
                k_c = k[cc * GLA_CHUNK:(cc + 1) * GLA_CHUNK, :]
                kd_s[rows, :] = (k_c * jnp.exp(-b)).astype(BF16)
                ke_s[rows, :] = (k_c * jnp.exp(b[GLA_CHUNK - 1:GLA_CHUNK, :] - b)).astype(BF16)

        def act_tile(dst_s, col, cs):
            dst_s[sub, cs] = _silu(proj(col + cs.start, PROJ_TILE).astype(BF16))

        def v_tile(cs):
            v_s[sub, cs] = proj(COL_V + cs.start, PROJ_TILE).astype(BF16)

        for rs in row_chunks:
            tasks.append(lambda rs=rs: prenorm(rs))
        for rs in row_chunks:
            tasks.append(lambda rs=rs: forget_gate(rs))
        for col in range(0, SSD_CONV_DIM, PROJ_TILE):
            cs = slice(col, col + PROJ_TILE)
            tasks.append(lambda cs=cs: xbc_tile(cs))
            for rs in row_chunks:
                tasks.append(lambda rs=rs, cs=cs: conv(rs, cs))
        def keep_conv_tail():
            xbc_s[0:CONV_PAD, :] = xbc_s[SUB_BLOCK:SUB_BLOCK + CONV_PAD, :]

        tasks.append(keep_conv_tail)
        ssd_chunks = range(r0 // SSD_CHUNK, (r0 + SUB_BLOCK) // SSD_CHUNK)
        ssd_shared = {ci: {} for ci in ssd_chunks}
        for ci in ssd_chunks:
            tasks.append(lambda ci=ci: ssd_decays(ci, ssd_shared[ci]))
        tasks.append(q_tile)
        tasks.append(k_tile)
        for ci in ssd_chunks:
            for g in range(SSD_GROUPS):
                tasks.append(lambda ci=ci, g=g: ssd_weights(ci, g, ssd_shared[ci]))
        for ci in range(r0 // GLA_CHUNK, (r0 + SUB_BLOCK) // GLA_CHUNK):
            tasks.append(lambda ci=ci: gla_scores(ci))
        for col in range(0, GLA_VAL, PROJ_TILE):
            cs = slice(col, col + PROJ_TILE)
            tasks.append(lambda cs=cs: act_tile(gact_s, COL_G, cs))
        for col in range(0, SSD_INNER, PROJ_TILE):
            cs = slice(col, col + PROJ_TILE)
            tasks.append(lambda cs=cs: act_tile(zact_s, COL_Z, cs))
        for col in range(0, GLA_VAL, PROJ_TILE):
            cs = slice(col, col + PROJ_TILE)
            tasks.append(lambda cs=cs: v_tile(cs))
        return tasks

    def gla_scores(ci):
        rows = slice(ci * GLA_CHUNK, (ci + 1) * GLA_CHUNK)
        for hh in range(GLA_HEADS):
            ks = slice(hh * GLA_DK, (hh + 1) * GLA_DK)
            att = _dot_nt(qd_s[rows, ks], kd_s[rows, ks])
            att_s[hh, rows, :] = jnp.where(causal_g, att, 0.0).astype(BF16)

    def gla_chunk(ci):
        rows = slice(ci * GLA_CHUNK, (ci + 1) * GLA_CHUNK)
        decay = dec_s[ci:ci + 1, :]
        for hh in range(GLA_HEADS):
            ks = slice(hh * GLA_DK, (hh + 1) * GLA_DK)
            vs = slice(hh * GLA_DV, (hh + 1) * GLA_DV)
            v_h = v_s[rows, vs]
            o = _dot(att_s[hh, rows, :], v_h) + _dot_nt(qd_s[rows, ks], gstate_b_s[hh])
            state_t = gstate_s[hh] * decay[:, ks] + _dot_tn(v_h, ke_s[rows, ks])
            gstate_s[hh] = state_t
            gstate_b_s[hh] = state_t.astype(BF16)
            o = (o * lax.rsqrt(jnp.mean(o * o, axis=-1, keepdims=True) + NORM_EPS)) * gnw_ref[...]
            mix_s[rows, vs] = (o * gact_s[rows, vs].astype(F32)).astype(BF16)

    causal_s, tril_s = _tril(SSD_CHUNK, BF16)
    tril_s2 = jnp.concatenate([tril_s, tril_s], axis=1)
    a_neg = -jnp.exp(alog_ref[...])

    def expand_heads(v):
        hi, lo = _split(v)
        return _dot(jnp.concatenate([hi, lo], axis=1), expand_s[...])

    def ssd_decays(ci, shared):
        rows = slice(ci * SSD_CHUNK, (ci + 1) * SSD_CHUNK)
        dt_in = sm_s[rows, :] + dtb_ref[...]
        dt = jnp.maximum(dt_in, 0.0) + _log1p_exp_neg_abs(dt_in)
        dta_hi, dta_lo = _split(dt * a_neg)
        a_cum = _dot(tril_s2, jnp.concatenate([dta_hi, dta_lo], axis=0))
        a_last = a_cum[SSD_CHUNK - 1:SSD_CHUNK, :]
        decay_in_x = expand_heads(jnp.exp(a_cum))
        decin_s[rows, :] = decay_in_x
        cdec_s[ci:ci + 1, :] = decay_in_x[SSD_CHUNK - 1:SSD_CHUNK, :]
        xss_s[rows, :] = (u_s[rows, :] * expand_heads(jnp.exp(a_last - a_cum) * dt)).astype(BF16)
        shared.update(a_cum=a_cum, src_t=(a_cum - jnp.log(dt)).T)

    def ssd_weights(ci, g, shared):
        rows = slice(ci * SSD_CHUNK, (ci + 1) * SSD_CHUNK)
        b_g = bc_s[rows, g * SSD_STATE:(g + 1) * SSD_STATE]
        c_g = bc_s[rows, (SSD_GROUPS + g) * SSD_STATE:(SSD_GROUPS + g + 1) * SSD_STATE]
        a_cum, src_t = shared["a_cum"], shared["src_t"]
        cb = _dot_nt(c_g, b_g)
        for hl in range(SSD_HEADS_PER_GROUP):
            head = g * SSD_HEADS_PER_GROUP + hl
            lane = DT_LANE + head
            seg = jnp.where(causal_s, a_cum[:, lane:lane + 1] - src_t[lane:lane + 1, :], MASKED_LOG)
            w_s[rows, head * SSD_CHUNK:(head + 1) * SSD_CHUNK] = (jnp.exp(seg) * cb).astype(BF16)

    def ssd_group(ci, g):
        rows = slice(ci * SSD_CHUNK, (ci + 1) * SSD_CHUNK)
        gs = slice(g * SSD_GROUP_WIDTH, (g + 1) * SSD_GROUP_WIDTH)
        b_g = bc_s[rows, g * SSD_STATE:(g + 1) * SSD_STATE]
        c_g = bc_s[rows, (SSD_GROUPS + g) * SSD_STATE:(SSD_GROUPS + g + 1) * SSD_STATE]
        y_g = _dot(c_g, sstate_b_s[g]) * decin_s[rows, gs]
        state = sstate_s[g] * cdec_s[ci:ci + 1, gs] + _dot_tn(b_g, xss_s[rows, gs])
        sstate_s[g] = state
        sstate_b_s[g] = state.astype(BF16)
        y_tiles = []
        for pair in range(SSD_HEADS_PER_GROUP // 2):
            ts = slice(gs.start + pair * LANES, gs.start + (pair + 1) * LANES)
            head = g * SSD_HEADS_PER_GROUP + 2 * pair
            x_pair = jnp.concatenate([xlo_s[rows, ts], xhi_s[rows, ts]], axis=0)
            y_tiles.append(_dot(w_s[rows, head * SSD_CHUNK:(head + 2) * SSD_CHUNK], x_pair))
        y_g = y_g + jnp.concatenate(y_tiles, axis=1) + dskip_ref[:, gs] * u_s[rows, gs]
        y_g = y_g * zact_s[rows, gs].astype(F32)
        y_g = (y_g * lax.rsqrt(jnp.mean(y_g * y_g, axis=-1, keepdims=True) + NORM_EPS)) * snw_ref[:, gs]
        mix_s[rows, GLA_VAL + g * SSD_GROUP_WIDTH:GLA_VAL + (g + 1) * SSD_GROUP_WIDTH] = y_g.astype(BF16)

    def out_rows(rs):
        mixed = _dot(mix_s[rs, :], wout_ref[...])
        mixed = (mixed * lax.rsqrt(jnp.mean(mixed * mixed, axis=-1, keepdims=True) + NORM_EPS)) * postw_ref[...]
        out_ref[0, rs, :] = x_ref[0, rs, :] + gate * mixed

    def recur_tasks(r0):
        tasks = []
        gla_per_ssd = SSD_CHUNK // GLA_CHUNK
        for ci in range(r0 // SSD_CHUNK, (r0 + SUB_BLOCK) // SSD_CHUNK):
            for cc in range(gla_per_ssd):
                tasks.append(lambda ci=ci, cc=cc: gla_chunk(ci * gla_per_ssd + cc))
            for g in range(SSD_GROUPS):
                tasks.append(lambda ci=ci, g=g: ssd_group(ci, g))
        tasks.append(lambda: out_rows(slice(r0, r0 + SUB_BLOCK)))
        return tasks

    def issue_alternating(a, b):
        order = sorted([((i + 0.5) / len(a), 0, i) for i in range(len(a))]
                       + [((i + 0.5) / len(b), 1, i) for i in range(len(b))])
        for _, which, i in order:
            (a, b)[which][i]()

    sub_starts = list(range(0, tb, SUB_BLOCK))
    for task in project_tasks(sub_starts[0]):
        task()
    for prev, cur in zip(sub_starts[:-1], sub_starts[1:]):
        issue_alternating(project_tasks(cur), recur_tasks(prev))
    for task in recur_tasks(sub_starts[-1]):
        task()


def _const_spec(shape):
    zeros = (0,) * len(shape)
    return pl.BlockSpec(shape, lambda b, t: zeros, pipeline_mode=pl.Buffered(1))


def _layer(x, ada, vecs, wmain, wsm, w2, wout):
    bsz, seq, _ = x.shape
    tb = TIME_BLOCK
    consts = (ada, vecs, wmain, wsm, w2, wout)
    return pl.pallas_call(
        _layer_kernel,
        grid=(bsz, seq // tb),
        in_specs=[pl.BlockSpec((1, tb, D_MODEL), lambda b, t: (b, t, 0))]
        + [_const_spec(a.shape) for a in consts],
        out_specs=pl.BlockSpec((1, tb, D_MODEL), lambda b, t: (b, t, 0)),
        out_shape=jax.ShapeDtypeStruct(x.shape, x.dtype),
        scratch_shapes=[
            pltpu.VMEM((tb, D_MODEL), BF16),
            pltpu.VMEM((tb, LANES), F32),
            pltpu.VMEM((SUB_BLOCK + CONV_PAD, SSD_CONV_DIM), F32),
            pltpu.VMEM((tb, SSD_INNER), F32),
            pltpu.VMEM((tb, GLA_KEY), F32),
            pltpu.VMEM((tb, GLA_KEY), BF16),
            pltpu.VMEM((tb, GLA_KEY), BF16),
            pltpu.VMEM((tb, GLA_KEY), BF16),
            pltpu.VMEM((tb // GLA_CHUNK, GLA_KEY), F32),
            pltpu.VMEM((tb, GLA_VAL), BF16),
            pltpu.VMEM((tb, GLA_VAL), BF16),
            pltpu.VMEM((tb, SSD_INNER), BF16),
            pltpu.VMEM((tb, D_MIX), BF16),
            pltpu.VMEM((GLA_HEADS, GLA_DV, GLA_DK), F32),
            pltpu.VMEM((SSD_GROUPS, SSD_STATE, SSD_GROUP_WIDTH), F32),
            pltpu.VMEM((2 * LANES, GLA_KEY), BF16),
            pltpu.VMEM((LANES, GLA_KEY), BF16),
            pltpu.VMEM((2 * LANES, SSD_INNER), BF16),
            pltpu.VMEM((tb, SSD_INNER), BF16),
            pltpu.VMEM((tb, SSD_INNER), BF16),
            pltpu.VMEM((tb, 2 * SSD_GROUPS * SSD_STATE), BF16),
            pltpu.VMEM((GLA_HEADS, GLA_DV, GLA_DK), BF16),
            pltpu.VMEM((SSD_GROUPS, SSD_STATE, SSD_GROUP_WIDTH), BF16),
            pltpu.VMEM((GLA_HEADS, tb, GLA_CHUNK), BF16),
            pltpu.VMEM((tb, SSD_HEADS * SSD_CHUNK), BF16),
            pltpu.VMEM((tb, SSD_INNER), BF16),
            pltpu.VMEM((tb, SSD_INNER), F32),
            pltpu.VMEM((tb // SSD_CHUNK, SSD_INNER), F32),
        ],
        compiler_params=pltpu.CompilerParams(
            dimension_semantics=("arbitrary", "arbitrary"),
            vmem_limit_bytes=VMEM_LIMIT_BYTES),
        name="hybrid_layer",
    )(x, *consts)


def _prep_kernel(wt_ref, gate_ref, dt_ref, wmain_ref, wsm_ref):
    wmain_ref[...] = wt_ref[...].T.astype(BF16)

    @pl.when(pl.program_id(0) == 0)
    def _():
        narrow_t = jnp.concatenate(
            [gate_ref[...], dt_ref[...], jnp.zeros((LANES - DT_LANE - SSD_HEADS, D_MODEL), F32)], axis=0)
        hi, lo = _split(narrow_t.T)
        wsm_ref[:, 0:LANES] = hi
        wsm_ref[:, LANES:2 * LANES] = lo


def _prep_weights(w_t):
    def src_row(j):
        return pl.multiple_of(j * PROJ_TILE + jnp.where(j * PROJ_TILE >= IN_GATE, GLA_GATE_RANK, 0), SUBLANES)

    return pl.pallas_call(
        _prep_kernel,
        grid=(W_MAIN_COLS // PROJ_TILE,),
        in_specs=[pl.BlockSpec((pl.Element(PROJ_TILE), pl.Element(D_MODEL)), lambda j: (src_row(j), 0)),
                  pl.BlockSpec((pl.Element(GLA_GATE_RANK), pl.Element(D_MODEL)), lambda j: (IN_GATE, 0)),
                  pl.BlockSpec((pl.Element(SSD_HEADS), pl.Element(D_MODEL)), lambda j: (IN_DT, 0))],
        out_specs=[pl.BlockSpec((D_MODEL, PROJ_TILE), lambda j: (0, j)),
                   pl.BlockSpec((D_MODEL, 2 * LANES), lambda j: (0, 0))],
        out_shape=[jax.ShapeDtypeStruct((D_MODEL, W_MAIN_COLS), BF16),
                   jax.ShapeDtypeStruct((D_MODEL, 2 * LANES), BF16)],
        compiler_params=pltpu.CompilerParams(dimension_semantics=("arbitrary",)),
        name="prep_weights",
    )(w_t, w_t, w_t)


def _row(v, offset=0):
    tail = VEC_COLS - offset - v.shape[0]
    return [jnp.zeros((offset,), F32), v.astype(F32), jnp.zeros((tail,), F32)]


def kernel(x, c, ada_w, ada_b, pre_norm_w, w_in, gla_gate_w2, gla_gate_b, gla_norm_w, conv_w, conv_b,
           dt_bias, a_log, d_skip, ssd_norm_w, w_out, post_norm_w):
    bsz, seq, _ = x.shape
    assert seq % TIME_BLOCK == 0 and bsz <= SUBLANES
    depth = ada_w.shape[0]
    c_pad = jnp.pad(c, ((0, SUBLANES - bsz), (0, 0)))
    for i in range(depth):
        ada = _ada(c_pad, ada_w[i], ada_b[i][None, :])
        wmain, wsm2 = _prep_weights(w_in[i].T)
        rows = ([_row(pre_norm_w[i]), _row(gla_gate_b[i]), _row(gla_norm_w[i]), _row(conv_b[i])]
                + [_row(conv_w[i, kk]) for kk in range(SSD_CONV)]
                + [_row(dt_bias[i], DT_LANE), _row(a_log[i], DT_LANE),
                   _row(jnp.repeat(d_skip[i], SSD_HEAD_DIM)), _row(ssd_norm_w[i]), _row(post_norm_w[i])]
                + [[jnp.zeros(((VEC_ROWS - VEC_USED) * VEC_COLS,), F32)]])
        vecs = jnp.concatenate([piece for row in rows for piece in row]).reshape(VEC_ROWS, VEC_COLS)
        w2 = jnp.pad(gla_gate_w2[i], ((0, LANES - GLA_GATE_RANK), (0, 0)))
        x = _layer(x, ada, vecs, wmain, wsm2, w2, w_out[i].astype(BF16))
    return x
```

```python
import jax
import jax.numpy as jnp
from jax import lax
from jax.experimental import pallas as pl
from jax.experimental.pallas import tpu as pltpu

F32 = jnp.float32
BF16 = jnp.bfloat16

NORM_EPS = 1e-6
D_MODEL = 1024
GLA_HEADS = 4
GLA_DK = 128
GLA_DV = 256
GLA_KEY = GLA_HEADS * GLA_DK
GLA_VAL = GLA_HEADS * GLA_DV
GLA_GATE_RANK = 16
GLA_TAU = 16.0
GLA_CHUNK = 64
SSD_INNER = 1024
SSD_HEAD_DIM = 64
SSD_HEADS = 16
SSD_GROUPS = 2
SSD_HEADS_PER_GROUP = SSD_HEADS // SSD_GROUPS
SSD_GROUP_WIDTH = SSD_INNER // SSD_GROUPS
SSD_STATE = 128
SSD_CONV = 4
SSD_CHUNK = 128
SSD_CONV_DIM = SSD_INNER + 2 * SSD_GROUPS * SSD_STATE
D_MIX = GLA_VAL + SSD_INNER

LANES = 128
SUBLANES = 8

TIME_BLOCK = 1024
ROW_CHUNK = 128
PROJ_TILE = 512
SUB_BLOCK = 256
COL_QK = 0
COL_V = COL_QK + 2 * GLA_KEY
COL_G = COL_V + GLA_VAL
COL_Z = COL_G + GLA_VAL
COL_XBC = COL_Z + SSD_INNER
W_MAIN_COLS = COL_XBC + SSD_CONV_DIM
DT_LANE = GLA_GATE_RANK
assert DT_LANE + SSD_HEADS <= LANES
IN_GATE = 2 * GLA_KEY + 2 * GLA_VAL
IN_Z = IN_GATE + GLA_GATE_RANK
IN_DT = IN_Z + SSD_INNER + SSD_CONV_DIM
D_PROJ = IN_DT + SSD_HEADS
assert IN_GATE % PROJ_TILE == 0 and W_MAIN_COLS % PROJ_TILE == 0 and SSD_INNER % PROJ_TILE == 0
assert 2 * SSD_HEAD_DIM == LANES
MASKED_LOG = -1e30
(VEC_PRE_W, VEC_GATE_B, VEC_GLA_NORM, VEC_CONV_B, VEC_CONV_W, VEC_DT_BIAS, VEC_A_LOG,
 VEC_D_SKIP, VEC_SSD_NORM, VEC_POST_W) = (0, 1, 2, 3, 4, 4 + SSD_CONV, 5 + SSD_CONV, 6 + SSD_CONV,
                                          7 + SSD_CONV, 8 + SSD_CONV)
VEC_USED = VEC_POST_W + 1
VEC_ROWS = -(-VEC_USED // SUBLANES) * SUBLANES
VEC_COLS = SSD_CONV_DIM
CONV_PAD = SUBLANES
VMEM_LIMIT_BYTES = 62 * 1024 * 1024


def _dot(a, b):
    return jnp.dot(a, b, preferred_element_type=F32)


def _dot_nt(a, b):
    return lax.dot_general(a, b, (((1,), (1,)), ((), ())), preferred_element_type=F32)


def _dot_tn(a, b):
    return lax.dot_general(a, b, (((0,), (0,)), ((), ())), preferred_element_type=F32)


def _split(x):
    hi = x.astype(BF16)
    lo = (x - hi.astype(F32)).astype(BF16)
    return hi, lo


def _silu(x):
    hx = 0.5 * x
    return hx + hx * jnp.tanh(hx)


def _log1p_exp_neg_abs(x):
    return jnp.log(1.0 + jnp.exp(-jnp.abs(x)))


def _tril(n, dtype):
    r = lax.broadcasted_iota(jnp.int32, (n, n), 0)
    c = lax.broadcasted_iota(jnp.int32, (n, n), 1)
    return r >= c, (r >= c).astype(dtype)


def _ada_kernel(c_ref, w_ref, b_ref, o_ref):
    c = c_ref[...]
    ca_hi, ca_lo = _split(_silu(c))
    w_hi, w_lo = _split(w_ref[...])
    o_ref[...] = _dot(ca_hi, w_hi) + _dot(ca_hi, w_lo) + _dot(ca_lo, w_hi) + b_ref[...]


def _ada(c_pad, ada_w, ada_b):
    rows = c_pad.shape[0]
    return pl.pallas_call(
        _ada_kernel,
        grid=(3,),
        in_specs=[
            pl.BlockSpec((rows, D_MODEL), lambda j: (0, 0)),
            pl.BlockSpec((D_MODEL, D_MODEL), lambda j: (0, j)),
            pl.BlockSpec((1, D_MODEL), lambda j: (0, j)),
        ],
        out_specs=pl.BlockSpec((rows, D_MODEL), lambda j: (0, j)),
        out_shape=jax.ShapeDtypeStruct((rows, 3 * D_MODEL), F32),
        compiler_params=pltpu.CompilerParams(dimension_semantics=("arbitrary",)),
        name="adaln",
    )(c_pad, ada_w, ada_b)


def _layer_kernel(x_ref, ada_ref, vec_ref, wmain_ref, wsm_ref, w2_ref, wout_ref,
                  out_ref,
                  h_s, sm_s, xbc_s, u_s, b_s, qd_s, kd_s, ke_s, dec_s, v_s, gact_s, zact_s, mix_s,
                  gstate_s, sstate_s, w2a_s, expand_s, xlo_s, xhi_s, bc_s,
                  gstate_b_s, sstate_b_s, att_s, w_s, xss_s, decin_s, cdec_s):
    tb = TIME_BLOCK
    t = pl.program_id(1)

    @pl.when(t == 0)
    def _():
        gstate_s[...] = jnp.zeros_like(gstate_s)
        sstate_s[...] = jnp.zeros_like(sstate_s)
        gstate_b_s[...] = jnp.zeros_like(gstate_b_s)
        sstate_b_s[...] = jnp.zeros_like(sstate_b_s)
        xbc_s[0:CONV_PAD, :] = jnp.zeros((CONV_PAD, SSD_CONV_DIM), F32)
        w2_hi = w2_ref[...].astype(BF16)
        w2a_s[0:LANES, :] = w2_hi
        w2a_s[LANES:2 * LANES, :] = w2_hi
        head_row = lax.broadcasted_iota(jnp.int32, (2 * LANES, SSD_INNER), 0) % LANES
        head_of_col = lax.broadcasted_iota(jnp.int32, (2 * LANES, SSD_INNER), 1) // SSD_HEAD_DIM + DT_LANE
        expand_s[...] = (head_row == head_of_col).astype(BF16)

    def vec_view(row, width, rows=1):
        return vec_ref.at[row:row + rows, 0:width]

    prew_ref = vec_view(VEC_PRE_W, D_MODEL)
    gb_ref = vec_view(VEC_GATE_B, GLA_KEY)
    gnw_ref = vec_view(VEC_GLA_NORM, GLA_DV)
    cb_ref = vec_view(VEC_CONV_B, SSD_CONV_DIM)
    cw_ref = vec_view(VEC_CONV_W, SSD_CONV_DIM, SSD_CONV)
    dtb_ref = vec_view(VEC_DT_BIAS, LANES)
    alog_ref = vec_view(VEC_A_LOG, LANES)
    dskip_ref = vec_view(VEC_D_SKIP, SSD_INNER)
    snw_ref = vec_view(VEC_SSD_NORM, SSD_INNER)
    postw_ref = vec_view(VEC_POST_W, D_MODEL)

    ada = ada_ref[pl.ds(pl.program_id(0), 1), :]
    shift = ada[:, 0:D_MODEL]
    scale1 = 1.0 + ada[:, D_MODEL:2 * D_MODEL]
    gate = ada[:, 2 * D_MODEL:3 * D_MODEL]
    causal_g, tril_g = _tril(GLA_CHUNK, BF16)
    tril_g2 = jnp.concatenate([tril_g, tril_g], axis=1)

    def project_tasks(r0):
        sub = slice(r0, r0 + SUB_BLOCK)
        row_chunks = [slice(r, r + ROW_CHUNK) for r in range(r0, r0 + SUB_BLOCK, ROW_CHUNK)]
        tasks = []

        def proj(col, width):
            return _dot(h_s[sub, :], wmain_ref[:, col:col + width])

        def prenorm(rs):
            x = x_ref[0, rs, :]
            ms = jnp.mean(x * x, axis=-1, keepdims=True)
            h = (x * lax.rsqrt(ms + NORM_EPS)) * prew_ref[...] * scale1 + shift
            h_hi = h.astype(BF16)
            h_s[rs, :] = h_hi
            both = _dot(h_hi, wsm_ref[...])
            sm_s[rs, :] = both[:, 0:LANES] + both[:, LANES:2 * LANES]

        def forget_gate(rs):
            sm_hi, sm_lo = _split(sm_s[rs, :])
            sm_both = jnp.concatenate([sm_hi, sm_lo], axis=1)
            gl = _dot(sm_both, w2a_s[...]) + gb_ref[...]
            la = (jnp.minimum(gl, 0.0) - _log1p_exp_neg_abs(gl)) * (1.0 / GLA_TAU)
            for cc in range(ROW_CHUNK // GLA_CHUNK):
                la_hi, la_lo = _split(la[cc * GLA_CHUNK:(cc + 1) * GLA_CHUNK, :])
                b = _dot(tril_g2, jnp.concatenate([la_hi, la_lo], axis=0))
                c0 = rs.start + cc * GLA_CHUNK
                b_s[c0:c0 + GLA_CHUNK, :] = b
                ci = c0 // GLA_CHUNK
                dec_s[ci:ci + 1, :] = jnp.exp(b[GLA_CHUNK - 1:GLA_CHUNK, :])

        def xbc_tile(cs):
            xbc_s[CONV_PAD:CONV_PAD + SUB_BLOCK, cs] = proj(COL_XBC + cs.start, PROJ_TILE)

        def conv(rs, cs):
            a = xbc_s[rs.start - r0:rs.start - r0 + ROW_CHUNK + CONV_PAD, cs]
            acc = cw_ref[0:1, cs] * a
            for kk in range(1, SSD_CONV):
                acc = pltpu.roll(acc, 1, 0) + cw_ref[kk:kk + 1, cs] * a
            u = _silu(acc[CONV_PAD:, :] + cb_ref[:, cs])
            if cs.start < SSD_INNER:
                u_s[rs, cs] = u
                first_head = lax.broadcasted_iota(jnp.int32, u.shape, 1) % LANES < SSD_HEAD_DIM
                xlo_s[rs, cs] = jnp.where(first_head, u, 0.0).astype(BF16)
                xhi_s[rs, cs] = jnp.where(first_head, 0.0, u).astype(BF16)
            else:
                bc_s[rs, cs.start - SSD_INNER:cs.stop - SSD_INNER] = u.astype(BF16)

        def q_tile():
            q = proj(COL_QK, GLA_KEY)
            qd_s[sub, :] = (q * (GLA_DK ** -0.5) * jnp.exp(b_s[sub, :])).astype(BF16)

        def k_tile():
            k = proj(COL_QK + GLA_KEY, GLA_KEY)
            for cc in range(SUB_BLOCK // GLA_CHUNK):
                rows = slice(r0 + cc * GLA_CHUNK, r0 + (cc + 1) * GLA_CHUNK)
                b = b_s[rows, :]
                k_c = k[cc * GLA_CHUNK:(cc + 1) * GLA_CHUNK, :]
                kd_s[rows, :] = (k_c * jnp.exp(-b)).astype(BF16)
                ke_s[rows, :] = (k_c * jnp.exp(b[GLA_CHUNK - 1:GLA_CHUNK, :] - b)).astype(BF16)

        def act_tile(dst_s, col, cs):
            dst_s[sub, cs] = _silu(proj(col + cs.start, PROJ_TILE).astype(BF16))

        def v_tile(cs):
            v_s[sub, cs] = proj(COL_V + cs.start, PROJ_TILE).astype(BF16)

        for rs in row_chunks:
            tasks.append(lambda rs=rs: prenorm(rs))
        for rs in row_chunks:
            tasks.append(lambda rs=rs: forget_gate(rs))
        for col in range(0, SSD_CONV_DIM, PROJ_TILE):
            cs = slice(col, col + PROJ_TILE)
            tasks.append(lambda cs=cs: xbc_tile(cs))
            for rs in row_chunks:
                tasks.append(lambda rs=rs, cs=cs: conv(rs, cs))
        def keep_conv_tail():
            xbc_s[0:CONV_PAD, :] = xbc_s[SUB_BLOCK:SUB_BLOCK + CONV_PAD, :]

        tasks.append(keep_conv_tail)
        ssd_chunks = range(r0 // SSD_CHUNK, (r0 + SUB_BLOCK) // SSD_CHUNK)
        ssd_shared = {ci: {} for ci in ssd_chunks}
        for ci in ssd_chunks:
            tasks.append(lambda ci=ci: ssd_decays(ci, ssd_shared[ci]))
        tasks.append(q_tile)
        tasks.append(k_tile)
        for ci in ssd_chunks:
            for g in range(SSD_GROUPS):
                tasks.append(lambda ci=ci, g=g: ssd_weights(ci, g, ssd_shared[ci]))
        for ci in range(r0 // GLA_CHUNK, (r0 + SUB_BLOCK) // GLA_CHUNK):
            tasks.append(lambda ci=ci: gla_scores(ci))
        for col in range(0, GLA_VAL, PROJ_TILE):
            cs = slice(col, col + PROJ_TILE)
            tasks.append(lambda cs=cs: act_tile(gact_s, COL_G, cs))
        for col in range(0, SSD_INNER, PROJ_TILE):
            cs = slice(col, col + PROJ_TILE)
            tasks.append(lambda cs=cs: act_tile(zact_s, COL_Z, cs))
        for col in range(0, GLA_VAL, PROJ_TILE):
            cs = slice(col, col + PROJ_TILE)
            tasks.append(lambda cs=cs: v_tile(cs))
        return tasks

    def gla_scores(ci):
        rows = slice(ci * GLA_CHUNK, (ci + 1) * GLA_CHUNK)
        for hh in range(GLA_HEADS):
            ks = slice(hh * GLA_DK, (hh + 1) * GLA_DK)
            att = _dot_nt(qd_s[rows, ks], kd_s[rows, ks])
            att_s[hh, rows, :] = jnp.where(causal_g, att, 0.0).astype(BF16)

    def gla_chunk(ci):
        rows = slice(ci * GLA_CHUNK, (ci + 1) * GLA_CHUNK)
        decay = dec_s[ci:ci + 1, :]
        for hh in range(GLA_HEADS):
            ks = slice(hh * GLA_DK, (hh + 1) * GLA_DK)
            vs = slice(hh * GLA_DV, (hh + 1) * GLA_DV)
            v_h = v_s[rows, vs]
            o = _dot(att_s[hh, rows, :], v_h) + _dot_nt(qd_s[rows, ks], gstate_b_s[hh])
            state_t = gstate_s[hh] * decay[:, ks] + _dot_tn(v_h, ke_s[rows, ks])
            gstate_s[hh] = state_t
            gstate_b_s[hh] = state_t.astype(BF16)
            o = (o * lax.rsqrt(jnp.mean(o * o, axis=-1, keepdims=True) + NORM_EPS)) * gnw_ref[...]
            mix_s[rows, vs] = (o * gact_s[rows, vs].astype(F32)).astype(BF16)

    causal_s, tril_s = _tril(SSD_CHUNK, BF16)
    tril_s2 = jnp.concatenate([tril_s, tril_s], axis=1)
    a_neg = -jnp.exp(alog_ref[...])

    def expand_heads(v):
        hi, lo = _split(v)
        return _dot(jnp.concatenate([hi, lo], axis=1), expand_s[...])

    def ssd_decays(ci, shared):
        rows = slice(ci * SSD_CHUNK, (ci + 1) * SSD_CHUNK)
        dt_in = sm_s[rows, :] + dtb_ref[...]
        dt = jnp.maximum(dt_in, 0.0) + _log1p_exp_neg_abs(dt_in)
        dta_hi, dta_lo = _split(dt * a_neg)
        a_cum = _dot(tril_s2, jnp.concatenate([dta_hi, dta_lo], axis=0))
        a_last = a_cum[SSD_CHUNK - 1:SSD_CHUNK, :]
        decay_in_x = expand_heads(jnp.exp(a_cum))
        decin_s[rows, :] = decay_in_x
        cdec_s[ci:ci + 1, :] = decay_in_x[SSD_CHUNK - 1:SSD_CHUNK, :]
        xss_s[rows, :] = (u_s[rows, :] * expand_heads(jnp.exp(a_last - a_cum) * dt)).astype(BF16)
        shared.update(a_cum=a_cum, src_t=(a_cum - jnp.log(dt)).T)

    def ssd_weights(ci, g, shared):
        rows = slice(ci * SSD_CHUNK, (ci + 1) * SSD_CHUNK)
        b_g = bc_s[rows, g * SSD_STATE:(g + 1) * SSD_STATE]
        c_g = bc_s[rows, (SSD_GROUPS + g) * SSD_STATE:(SSD_GROUPS + g + 1) * SSD_STATE]
        a_cum, src_t = shared["a_cum"], shared["src_t"]
        cb = _dot_nt(c_g, b_g)
        for hl in range(SSD_HEADS_PER_GROUP):
            head = g * SSD_HEADS_PER_GROUP + hl
            lane = DT_LANE + head
            seg = jnp.where(causal_s, a_cum[:, lane:lane + 1] - src_t[lane:lane + 1, :], MASKED_LOG)
            w_s[rows, head * SSD_CHUNK:(head + 1) * SSD_CHUNK] = (jnp.exp(seg) * cb).astype(BF16)

    def ssd_group(ci, g):
        rows = slice(ci * SSD_CHUNK, (ci + 1) * SSD_CHUNK)
        gs = slice(g * SSD_GROUP_WIDTH, (g + 1) * SSD_GROUP_WIDTH)
        b_g = bc_s[rows, g * SSD_STATE:(g + 1) * SSD_STATE]
        c_g = bc_s[rows, (SSD_GROUPS + g) * SSD_STATE:(SSD_GROUPS + g + 1) * SSD_STATE]
        y_g = _dot(c_g, sstate_b_s[g]) * decin_s[rows, gs]
        state = sstate_s[g] * cdec_s[ci:ci + 1, gs] + _dot_tn(b_g, xss_s[rows, gs])
        sstate_s[g] = state
        sstate_b_s[g] = state.astype(BF16)
        y_tiles = []
        for pair in range(SSD_HEADS_PER_GROUP // 2):
            ts = slice(gs.start + pair * LANES, gs.start + (pair + 1) * LANES)
            head = g * SSD_HEADS_PER_GROUP + 2 * pair
            x_pair = jnp.concatenate([xlo_s[rows, ts], xhi_s[rows, ts]], axis=0)
            y_tiles.append(_dot(w_s[rows, head * SSD_CHUNK:(head + 2) * SSD_CHUNK], x_pair))
        y_g = y_g + jnp.concatenate(y_tiles, axis=1) + dskip_ref[:, gs] * u_s[rows, gs]
        y_g = y_g * zact_s[rows, gs].astype(F32)
        y_g = (y_g * lax.rsqrt(jnp.mean(y_g * y_g, axis=-1, keepdims=True) + NORM_EPS)) * snw_ref[:, gs]
        mix_s[rows, GLA_VAL + g * SSD_GROUP_WIDTH:GLA_VAL + (g + 1) * SSD_GROUP_WIDTH] = y_g.astype(BF16)

    def out_rows(rs):
        mixed = _dot(mix_s[rs, :], wout_ref[...])
        mixed = (mixed * lax.rsqrt(jnp.mean(mixed * mixed, axis=-1, keepdims=True) + NORM_EPS)) * postw_ref[...]
        out_ref[0, rs, :] = x_ref[0, rs, :] + gate * mixed

    def recur_tasks(r0):
        tasks = []
        gla_per_ssd = SSD_CHUNK // GLA_CHUNK
        for ci in range(r0 // SSD_CHUNK, (r0 + SUB_BLOCK) // SSD_CHUNK):
            for cc in range(gla_per_ssd):
                tasks.append(lambda ci=ci, cc=cc: gla_chunk(ci * gla_per_ssd + cc))
            for g in range(SSD_GROUPS):
                tasks.append(lambda ci=ci, g=g: ssd_group(ci, g))
        tasks.append(lambda: out_rows(slice(r0, r0 + SUB_BLOCK)))
        return tasks

    def issue_alternating(a, b):
        order = sorted([((i + 0.5) / len(a), 0, i) for i in range(len(a))]
                       + [((i + 0.5) / len(b), 1, i) for i in range(len(b))])
        for _, which, i in order:
            (a, b)[which][i]()

    sub_starts = list(range(0, tb, SUB_BLOCK))
    for task in project_tasks(sub_starts[0]):
        task()
    for prev, cur in zip(sub_starts[:-1], sub_starts[1:]):
        issue_alternating(project_tasks(cur), recur_tasks(prev))
    for task in recur_tasks(sub_starts[-1]):
        task()


def _const_spec(shape):
    zeros = (0,) * len(shape)
    return pl.BlockSpec(shape, lambda b, t: zeros, pipeline_mode=pl.Buffered(1))


def _layer(x, ada, vecs, wmain, wsm, w2, wout):
    bsz, seq, _ = x.shape
    tb = TIME_BLOCK
    consts = (ada, vecs, wmain, wsm, w2, wout)
    return pl.pallas_call(
        _layer_kernel,
        grid=(bsz, seq // tb),
        in_specs=[pl.BlockSpec((1, tb, D_MODEL), lambda b, t: (b, t, 0))]
        + [_const_spec(a.shape) for a in consts],
        out_specs=pl.BlockSpec((1, tb, D_MODEL), lambda b, t: (b, t, 0)),
        out_shape=jax.ShapeDtypeStruct(x.shape, x.dtype),
        scratch_shapes=[
            pltpu.VMEM((tb, D_MODEL), BF16),
            pltpu.VMEM((tb, LANES), F32),
            pltpu.VMEM((SUB_BLOCK + CONV_PAD, SSD_CONV_DIM), F32),
            pltpu.VMEM((tb, SSD_INNER), F32),
            pltpu.VMEM((tb, GLA_KEY), F32),
            pltpu.VMEM((tb, GLA_KEY), BF16),
            pltpu.VMEM((tb, GLA_KEY), BF16),
            pltpu.VMEM((tb, GLA_KEY), BF16),
            pltpu.VMEM((tb // GLA_CHUNK, GLA_KEY), F32),
            pltpu.VMEM((tb, GLA_VAL), BF16),
            pltpu.VMEM((tb, GLA_VAL), BF16),
            pltpu.VMEM((tb, SSD_INNER), BF16),
            pltpu.VMEM((tb, D_MIX), BF16),
            pltpu.VMEM((GLA_HEADS, GLA_DV, GLA_DK), F32),
            pltpu.VMEM((SSD_GROUPS, SSD_STATE, SSD_GROUP_WIDTH), F32),
            pltpu.VMEM((2 * LANES, GLA_KEY), BF16),
            pltpu.VMEM((2 * LANES, SSD_INNER), BF16),
            pltpu.VMEM((tb, SSD_INNER), BF16),
            pltpu.VMEM((tb, SSD_INNER), BF16),
            pltpu.VMEM((tb, 2 * SSD_GROUPS * SSD_STATE), BF16),
            pltpu.VMEM((GLA_HEADS, GLA_DV, GLA_DK), BF16),
            pltpu.VMEM((SSD_GROUPS, SSD_STATE, SSD_GROUP_WIDTH), BF16),
            pltpu.VMEM((GLA_HEADS, tb, GLA_CHUNK), BF16),
            pltpu.VMEM((tb, SSD_HEADS * SSD_CHUNK), BF16),
            pltpu.VMEM((tb, SSD_INNER), BF16),
            pltpu.VMEM((tb, SSD_INNER), F32),
            pltpu.VMEM((tb // SSD_CHUNK, SSD_INNER), F32),
        ],
        compiler_params=pltpu.CompilerParams(
            dimension_semantics=("arbitrary", "arbitrary"),
            vmem_limit_bytes=VMEM_LIMIT_BYTES),
        name="hybrid_layer",
    )(x, *consts)


def _prep_kernel(wt_ref, gate_ref, dt_ref, wmain_ref, wsm_ref):
    wmain_ref[...] = wt_ref[...].T.astype(BF16)

    @pl.when(pl.program_id(0) == 0)
    def _():
        narrow_t = jnp.concatenate(
            [gate_ref[...], dt_ref[...], jnp.zeros((LANES - DT_LANE - SSD_HEADS, D_MODEL), F32)], axis=0)
        hi, lo = _split(narrow_t.T)
        wsm_ref[:, 0:LANES] = hi
        wsm_ref[:, LANES:2 * LANES] = lo


def _prep_weights(w_t):
    def src_row(j):
        return pl.multiple_of(j * PROJ_TILE + jnp.where(j * PROJ_TILE >= IN_GATE, GLA_GATE_RANK, 0), SUBLANES)

    return pl.pallas_call(
        _prep_kernel,
        grid=(W_MAIN_COLS // PROJ_TILE,),
        in_specs=[pl.BlockSpec((pl.Element(PROJ_TILE), pl.Element(D_MODEL)), lambda j: (src_row(j), 0)),
                  pl.BlockSpec((pl.Element(GLA_GATE_RANK), pl.Element(D_MODEL)), lambda j: (IN_GATE, 0)),
                  pl.BlockSpec((pl.Element(SSD_HEADS), pl.Element(D_MODEL)), lambda j: (IN_DT, 0))],
        out_specs=[pl.BlockSpec((D_MODEL, PROJ_TILE), lambda j: (0, j)),
                   pl.BlockSpec((D_MODEL, 2 * LANES), lambda j: (0, 0))],
        out_shape=[jax.ShapeDtypeStruct((D_MODEL, W_MAIN_COLS), BF16),
                   jax.ShapeDtypeStruct((D_MODEL, 2 * LANES), BF16)],
        compiler_params=pltpu.CompilerParams(dimension_semantics=("arbitrary",)),
        name="prep_weights",
    )(w_t, w_t, w_t)


def _row(v, offset=0):
    tail = VEC_COLS - offset - v.shape[0]
    return [jnp.zeros((offset,), F32), v.astype(F32), jnp.zeros((tail,), F32)]


def kernel(x, c, ada_w, ada_b, pre_norm_w, w_in, gla_gate_w2, gla_gate_b, gla_norm_w, conv_w, conv_b,
           dt_bias, a_log, d_skip, ssd_norm_w, w_out, post_norm_w):
    bsz, seq, _ = x.shape
    assert seq % TIME_BLOCK == 0 and bsz <= SUBLANES
    depth = ada_w.shape[0]
    c_pad = jnp.pad(c, ((0, SUBLANES - bsz), (0, 0)))
    for i in range(depth):
        ada = _ada(c_pad, ada_w[i], ada_b[i][None, :])
        wmain, wsm2 = _prep_weights(w_in[i].T)
        rows = ([_row(pre_norm_w[i]), _row(gla_gate_b[i]), _row(gla_norm_w[i]), _row(conv_b[i])]
                + [_row(conv_w[i, kk]) for kk in range(SSD_CONV)]
                + [_row(dt_bias[i], DT_LANE), _row(a_log[i], DT_LANE),
                   _row(jnp.repeat(d_skip[i], SSD_HEAD_DIM)), _row(ssd_norm_w[i]), _row(post_norm_w[i])]
                + [[jnp.zeros(((VEC_ROWS - VEC_USED) * VEC_COLS,), F32)]])
        vecs = jnp.concatenate([piece for row in rows for piece in row]).reshape(VEC_ROWS, VEC_COLS)
        w2 = jnp.pad(gla_gate_w2[i], ((0, LANES - GLA_GATE_RANK), (0, 0)))
        x = _layer(x, ada, vecs, wmain, wsm2, w2, w_out[i].astype(BF16))
    return x
```

```python
import jax
import jax.numpy as jnp
from jax import lax
from jax.experimental import pallas as pl
from jax.experimental.pallas import tpu as pltpu

F32 = jnp.float32
BF16 = jnp.bfloat16

NORM_EPS = 1e-6
D_MODEL = 1024
GLA_HEADS = 4
GLA_DK = 128
GLA_DV = 256
GLA_KEY = GLA_HEADS * GLA_DK
GLA_VAL = GLA_HEADS * GLA_DV
GLA_GATE_RANK = 16
GLA_TAU = 16.0
GLA_CHUNK = 64
SSD_INNER = 1024
SSD_HEAD_DIM = 64
SSD_HEADS = 16
SSD_GROUPS = 2
SSD_HEADS_PER_GROUP = SSD_HEADS // SSD_GROUPS
SSD_GROUP_WIDTH = SSD_INNER // SSD_GROUPS
SSD_STATE = 128
SSD_CONV = 4
SSD_CHUNK = 128
SSD_CONV_DIM = SSD_INNER + 2 * SSD_GROUPS * SSD_STATE
D_MIX = GLA_VAL + SSD_INNER

LANES = 128
SUBLANES = 8

TIME_BLOCK = 1024
ROW_CHUNK = 128
PROJ_TILE = 512
SUB_BLOCK = 256
COL_QK = 0
COL_V = COL_QK + 2 * GLA_KEY
COL_G = COL_V + GLA_VAL
COL_Z = COL_G + GLA_VAL
COL_XBC = COL_Z + SSD_INNER
W_MAIN_COLS = COL_XBC + SSD_CONV_DIM
DT_LANE = GLA_GATE_RANK
assert DT_LANE + SSD_HEADS <= LANES
IN_GATE = 2 * GLA_KEY + 2 * GLA_VAL
IN_Z = IN_GATE + GLA_GATE_RANK
IN_DT = IN_Z + SSD_INNER + SSD_CONV_DIM
D_PROJ = IN_DT + SSD_HEADS
assert IN_GATE % PROJ_TILE == 0 and W_MAIN_COLS % PROJ_TILE == 0 and SSD_INNER % PROJ_TILE == 0
assert 2 * SSD_HEAD_DIM == LANES
MASKED_LOG = -1e30
(VEC_PRE_W, VEC_GATE_B, VEC_GLA_NORM, VEC_CONV_B, VEC_CONV_W, VEC_DT_BIAS, VEC_A_LOG,
 VEC_D_SKIP, VEC_SSD_NORM, VEC_POST_W) = (0, 1, 2, 3, 4, 4 + SSD_CONV, 5 + SSD_CONV, 6 + SSD_CONV,
                                          7 + SSD_CONV, 8 + SSD_CONV)
VEC_USED = VEC_POST_W + 1
VEC_ROWS = -(-VEC_USED // SUBLANES) * SUBLANES
VEC_COLS = SSD_CONV_DIM
CONV_PAD = SUBLANES
VMEM_LIMIT_BYTES = 62 * 1024 * 1024


def _dot(a, b):
    return jnp.dot(a, b, preferred_element_type=F32)


def _dot_nt(a, b):
    return lax.dot_general(a, b, (((1,), (1,)), ((), ())), preferred_element_type=F32)


def _dot_tn(a, b):
    return lax.dot_general(a, b, (((0,), (0,)), ((), ())), preferred_element_type=F32)


def _split(x):
    hi = x.astype(BF16)
    lo = (x - hi.astype(F32)).astype(BF16)
    return hi, lo


def _silu(x):
    hx = 0.5 * x
    return hx + hx * jnp.tanh(hx)


def _log1p_exp_neg_abs(x):
    return jnp.log(1.0 + jnp.exp(-jnp.abs(x)))


def _tril(n, dtype):
    r = lax.broadcasted_iota(jnp.int32, (n, n), 0)
    c = lax.broadcasted_iota(jnp.int32, (n, n), 1)
    return r >= c, (r >= c).astype(dtype)


def _ada_kernel(c_ref, w_ref, b_ref, o_ref):
    c = c_ref[...]
    ca_hi, ca_lo = _split(_silu(c))
    w_hi, w_lo = _split(w_ref[...])
    o_ref[...] = _dot(ca_hi, w_hi) + _dot(ca_hi, w_lo) + _dot(ca_lo, w_hi) + b_ref[...]


def _ada(c_pad, ada_w, ada_b):
    rows = c_pad.shape[0]
    return pl.pallas_call(
        _ada_kernel,
        grid=(3,),
        in_specs=[
            pl.BlockSpec((rows, D_MODEL), lambda j: (0, 0)),
            pl.BlockSpec((D_MODEL, D_MODEL), lambda j: (0, j)),
            pl.BlockSpec((1, D_MODEL), lambda j: (0, j)),
        ],
        out_specs=pl.BlockSpec((rows, D_MODEL), lambda j: (0, j)),
        out_shape=jax.ShapeDtypeStruct((rows, 3 * D_MODEL), F32),
        compiler_params=pltpu.CompilerParams(dimension_semantics=("arbitrary",)),
        name="adaln",
    )(c_pad, ada_w, ada_b)


def _layer_kernel(x_ref, ada_ref, vec_ref, wmain_ref, wsm_ref, w2_ref, wout_ref,
                  out_ref,
                  h_s, sm_s, xbc_s, u_s, b_s, qd_s, kd_s, ke_s, dec_s, v_s, gact_s, zact_s, mix_s,
                  gstate_s, sstate_s, w2a_s, w2b_s, expand_s, xlo_s, xhi_s, bc_s,
                  gstate_b_s, sstate_b_s, att_s, w_s, xss_s, decin_s, cdec_s):
    tb = TIME_BLOCK
    t = pl.program_id(1)

    @pl.when(t == 0)
    def _():
        gstate_s[...] = jnp.zeros_like(gstate_s)
        sstate_s[...] = jnp.zeros_like(sstate_s)
        gstate_b_s[...] = jnp.zeros_like(gstate_b_s)
        sstate_b_s[...] = jnp.zeros_like(sstate_b_s)
        xbc_s[0:CONV_PAD, :] = jnp.zeros((CONV_PAD, SSD_CONV_DIM), F32)
        w2_hi, w2_lo = _split(w2_ref[...])
        w2a_s[0:LANES, :] = w2_hi
        w2a_s[LANES:2 * LANES, :] = w2_hi
        w2b_s[...] = w2_lo
        head_row = lax.broadcasted_iota(jnp.int32, (2 * LANES, SSD_INNER), 0) % LANES
        head_of_col = lax.broadcasted_iota(jnp.int32, (2 * LANES, SSD_INNER), 1) // SSD_HEAD_DIM + DT_LANE
        expand_s[...] = (head_row == head_of_col).astype(BF16)

    def vec_view(row, width, rows=1):
        return vec_ref.at[row:row + rows, 0:width]

    prew_ref = vec_view(VEC_PRE_W, D_MODEL)
    gb_ref = vec_view(VEC_GATE_B, GLA_KEY)
    gnw_ref = vec_view(VEC_GLA_NORM, GLA_DV)
    cb_ref = vec_view(VEC_CONV_B, SSD_CONV_DIM)
    cw_ref = vec_view(VEC_CONV_W, SSD_CONV_DIM, SSD_CONV)
    dtb_ref = vec_view(VEC_DT_BIAS, LANES)
    alog_ref = vec_view(VEC_A_LOG, LANES)
    dskip_ref = vec_view(VEC_D_SKIP, SSD_INNER)
    snw_ref = vec_view(VEC_SSD_NORM, SSD_INNER)
    postw_ref = vec_view(VEC_POST_W, D_MODEL)

    ada = ada_ref[pl.ds(pl.program_id(0), 1), :]
    shift = ada[:, 0:D_MODEL]
    scale1 = 1.0 + ada[:, D_MODEL:2 * D_MODEL]
    gate = ada[:, 2 * D_MODEL:3 * D_MODEL]
    causal_g, tril_g = _tril(GLA_CHUNK, BF16)
    tril_g2 = jnp.concatenate([tril_g, tril_g], axis=1)

    def project_tasks(r0):
        sub = slice(r0, r0 + SUB_BLOCK)
        row_chunks = [slice(r, r + ROW_CHUNK) for r in range(r0, r0 + SUB_BLOCK, ROW_CHUNK)]
        tasks = []

        def proj(col, width):
            return _dot(h_s[sub, :], wmain_ref[:, col:col + width])

        def prenorm(rs):
            x = x_ref[0, rs, :]
            ms = jnp.mean(x * x, axis=-1, keepdims=True)
            h = (x * lax.rsqrt(ms + NORM_EPS)) * prew_ref[...] * scale1 + shift
            h_hi = h.astype(BF16)
            h_lo = (h - h_hi.astype(F32)).astype(BF16)
            h_s[rs, :] = h_hi
            both = _dot(h_hi, wsm_ref[...])
            sm_s[rs, :] = both[:, 0:LANES] + both[:, LANES:2 * LANES] + _dot(h_lo, wsm_ref[:, 0:LANES])

        def forget_gate(rs):
            sm_hi, sm_lo = _split(sm_s[rs, :])
            sm_both = jnp.concatenate([sm_hi, sm_lo], axis=1)
            gl = _dot(sm_both, w2a_s[...]) + _dot(sm_hi, w2b_s[...]) + gb_ref[...]
            la = (jnp.minimum(gl, 0.0) - _log1p_exp_neg_abs(gl)) * (1.0 / GLA_TAU)
            for cc in range(ROW_CHUNK // GLA_CHUNK):
                la_hi, la_lo = _split(la[cc * GLA_CHUNK:(cc + 1) * GLA_CHUNK, :])
                b = _dot(tril_g2, jnp.concatenate([la_hi, la_lo], axis=0))
                c0 = rs.start + cc * GLA_CHUNK
                b_s[c0:c0 + GLA_CHUNK, :] = b
                ci = c0 // GLA_CHUNK
                dec_s[ci:ci + 1, :] = jnp.exp(b[GLA_CHUNK - 1:GLA_CHUNK, :])

        def xbc_tile(cs):
            xbc_s[CONV_PAD:CONV_PAD + SUB_BLOCK, cs] = proj(COL_XBC + cs.start, PROJ_TILE)

        def conv(rs, cs):
            a = xbc_s[rs.start - r0:rs.start - r0 + ROW_CHUNK + CONV_PAD, cs]
            acc = cw_ref[0:1, cs] * a
            for kk in range(1, SSD_CONV):
                acc = pltpu.roll(acc, 1, 0) + cw_ref[kk:kk + 1, cs] * a
            u = _silu(acc[CONV_PAD:, :] + cb_ref[:, cs])
            if cs.start < SSD_INNER:
                u_s[rs, cs] = u
                first_head = lax.broadcasted_iota(jnp.int32, u.shape, 1) % LANES < SSD_HEAD_DIM
                xlo_s[rs, cs] = jnp.where(first_head, u, 0.0).astype(BF16)
                xhi_s[rs, cs] = jnp.where(first_head, 0.0, u).astype(BF16)
            else:
                bc_s[rs, cs.start - SSD_INNER:cs.stop - SSD_INNER] = u.astype(BF16)

        def q_tile():
            q = proj(COL_QK, GLA_KEY)
            qd_s[sub, :] = (q * (GLA_DK ** -0.5) * jnp.exp(b_s[sub, :])).astype(BF16)

        def k_tile():
            k = proj(COL_QK + GLA_KEY, GLA_KEY)
            for cc in range(SUB_BLOCK // GLA_CHUNK):
                rows = slice(r0 + cc * GLA_CHUNK, r0 + (cc + 1) * GLA_CHUNK)
                b = b_s[rows, :]
                k_c = k[cc * GLA_CHUNK:(cc + 1) * GLA_CHUNK, :]
                kd_s[rows, :] = (k_c * jnp.exp(-b)).astype(BF16)
                ke_s[rows, :] = (k_c * jnp.exp(b[GLA_CHUNK - 1:GLA_CHUNK, :] - b)).astype(BF16)

        def act_tile(dst_s, col, cs):
            dst_s[sub, cs] = _silu(proj(col + cs.start, PROJ_TILE).astype(BF16))

        def v_tile(cs):
            v_s[sub, cs] = proj(COL_V + cs.start, PROJ_TILE).astype(BF16)

        for rs in row_chunks:
            tasks.append(lambda rs=rs: prenorm(rs))
        for rs in row_chunks:
            tasks.append(lambda rs=rs: forget_gate(rs))
        for col in range(0, SSD_CONV_DIM, PROJ_TILE):
            cs = slice(col, col + PROJ_TILE)
            tasks.append(lambda cs=cs: xbc_tile(cs))
            for rs in row_chunks:
                tasks.append(lambda rs=rs, cs=cs: conv(rs, cs))
        def keep_conv_tail():
            xbc_s[0:CONV_PAD, :] = xbc_s[SUB_BLOCK:SUB_BLOCK + CONV_PAD, :]

        tasks.append(keep_conv_tail)
        ssd_chunks = range(r0 // SSD_CHUNK, (r0 + SUB_BLOCK) // SSD_CHUNK)
        ssd_shared = {ci: {} for ci in ssd_chunks}
        for ci in ssd_chunks:
            tasks.append(lambda ci=ci: ssd_decays(ci, ssd_shared[ci]))
        tasks.append(q_tile)
        tasks.append(k_tile)
        for ci in ssd_chunks:
            for g in range(SSD_GROUPS):
                tasks.append(lambda ci=ci, g=g: ssd_weights(ci, g, ssd_shared[ci]))
        for ci in range(r0 // GLA_CHUNK, (r0 + SUB_BLOCK) // GLA_CHUNK):
            tasks.append(lambda ci=ci: gla_scores(ci))
        for col in range(0, GLA_VAL, PROJ_TILE):
            cs = slice(col, col + PROJ_TILE)
            tasks.append(lambda cs=cs: act_tile(gact_s, COL_G, cs))
        for col in range(0, SSD_INNER, PROJ_TILE):
            cs = slice(col, col + PROJ_TILE)
            tasks.append(lambda cs=cs: act_tile(zact_s, COL_Z, cs))
        for col in range(0, GLA_VAL, PROJ_TILE):
            cs = slice(col, col + PROJ_TILE)
            tasks.append(lambda cs=cs: v_tile(cs))
        return tasks

    def gla_scores(ci):
        rows = slice(ci * GLA_CHUNK, (ci + 1) * GLA_CHUNK)
        for hh in range(GLA_HEADS):
            ks = slice(hh * GLA_DK, (hh + 1) * GLA_DK)
            att = _dot_nt(qd_s[rows, ks], kd_s[rows, ks])
            att_s[hh, rows, :] = jnp.where(causal_g, att, 0.0).astype(BF16)

    def gla_chunk(ci):
        rows = slice(ci * GLA_CHUNK, (ci + 1) * GLA_CHUNK)
        decay = dec_s[ci:ci + 1, :]
        for hh in range(GLA_HEADS):
            ks = slice(hh * GLA_DK, (hh + 1) * GLA_DK)
            vs = slice(hh * GLA_DV, (hh + 1) * GLA_DV)
            v_h = v_s[rows, vs]
            o = _dot(att_s[hh, rows, :], v_h) + _dot_nt(qd_s[rows, ks], gstate_b_s[hh])
            state_t = gstate_s[hh] * decay[:, ks] + _dot_tn(v_h, ke_s[rows, ks])
            gstate_s[hh] = state_t
            gstate_b_s[hh] = state_t.astype(BF16)
            o = (o * lax.rsqrt(jnp.mean(o * o, axis=-1, keepdims=True) + NORM_EPS)) * gnw_ref[...]
            mix_s[rows, vs] = (o * gact_s[rows, vs].astype(F32)).astype(BF16)

    causal_s, tril_s = _tril(SSD_CHUNK, BF16)
    tril_s2 = jnp.concatenate([tril_s, tril_s], axis=1)
    a_neg = -jnp.exp(alog_ref[...])

    def expand_heads(v):
        hi, lo = _split(v)
        return _dot(jnp.concatenate([hi, lo], axis=1), expand_s[...])

    def ssd_decays(ci, shared):
        rows = slice(ci * SSD_CHUNK, (ci + 1) * SSD_CHUNK)
        dt_in = sm_s[rows, :] + dtb_ref[...]
        dt = jnp.maximum(dt_in, 0.0) + _log1p_exp_neg_abs(dt_in)
        dta_hi, dta_lo = _split(dt * a_neg)
        a_cum = _dot(tril_s2, jnp.concatenate([dta_hi, dta_lo], axis=0))
        a_last = a_cum[SSD_CHUNK - 1:SSD_CHUNK, :]
        decay_in_x = expand_heads(jnp.exp(a_cum))
        decin_s[rows, :] = decay_in_x
        cdec_s[ci:ci + 1, :] = decay_in_x[SSD_CHUNK - 1:SSD_CHUNK, :]
        xss_s[rows, :] = (u_s[rows, :] * expand_heads(jnp.exp(a_last - a_cum) * dt)).astype(BF16)
        shared.update(a_cum=a_cum, src_t=(a_cum - jnp.log(dt)).T)

    def ssd_weights(ci, g, shared):
        rows = slice(ci * SSD_CHUNK, (ci + 1) * SSD_CHUNK)
        b_g = bc_s[rows, g * SSD_STATE:(g + 1) * SSD_STATE]
        c_g = bc_s[rows, (SSD_GROUPS + g) * SSD_STATE:(SSD_GROUPS + g + 1) * SSD_STATE]
        a_cum, src_t = shared["a_cum"], shared["src_t"]
        cb = _dot_nt(c_g, b_g)
        for hl in range(SSD_HEADS_PER_GROUP):
            head = g * SSD_HEADS_PER_GROUP + hl
            lane = DT_LANE + head
            seg = jnp.where(causal_s, a_cum[:, lane:lane + 1] - src_t[lane:lane + 1, :], MASKED_LOG)
            w_s[rows, head * SSD_CHUNK:(head + 1) * SSD_CHUNK] = (jnp.exp(seg) * cb).astype(BF16)

    def ssd_group(ci, g):
        rows = slice(ci * SSD_CHUNK, (ci + 1) * SSD_CHUNK)
        gs = slice(g * SSD_GROUP_WIDTH, (g + 1) * SSD_GROUP_WIDTH)
        b_g = bc_s[rows, g * SSD_STATE:(g + 1) * SSD_STATE]
        c_g = bc_s[rows, (SSD_GROUPS + g) * SSD_STATE:(SSD_GROUPS + g + 1) * SSD_STATE]
        y_g = _dot(c_g, sstate_b_s[g]) * decin_s[rows, gs]
        state = sstate_s[g] * cdec_s[ci:ci + 1, gs] + _dot_tn(b_g, xss_s[rows, gs])
        sstate_s[g] = state
        sstate_b_s[g] = state.astype(BF16)
        y_tiles = []
        for pair in range(SSD_HEADS_PER_GROUP // 2):
            ts = slice(gs.start + pair * LANES, gs.start + (pair + 1) * LANES)
            head = g * SSD_HEADS_PER_GROUP + 2 * pair
            x_pair = jnp.concatenate([xlo_s[rows, ts], xhi_s[rows, ts]], axis=0)
            y_tiles.append(_dot(w_s[rows, head * SSD_CHUNK:(head + 2) * SSD_CHUNK], x_pair))
        y_g = y_g + jnp.concatenate(y_tiles, axis=1) + dskip_ref[:, gs] * u_s[rows, gs]
        y_g = y_g * zact_s[rows, gs].astype(F32)
        y_g = (y_g * lax.rsqrt(jnp.mean(y_g * y_g, axis=-1, keepdims=True) + NORM_EPS)) * snw_ref[:, gs]
        mix_s[rows, GLA_VAL + g * SSD_GROUP_WIDTH:GLA_VAL + (g + 1) * SSD_GROUP_WIDTH] = y_g.astype(BF16)

    def out_rows(rs):
        mixed = _dot(mix_s[rs, :], wout_ref[...])
        mixed = (mixed * lax.rsqrt(jnp.mean(mixed * mixed, axis=-1, keepdims=True) + NORM_EPS)) * postw_ref[...]
        out_ref[0, rs, :] = x_ref[0, rs, :] + gate * mixed

    def recur_tasks(r0):
        tasks = []
        gla_per_ssd = SSD_CHUNK // GLA_CHUNK
        for ci in range(r0 // SSD_CHUNK, (r0 + SUB_BLOCK) // SSD_CHUNK):
            for cc in range(gla_per_ssd):
                tasks.append(lambda ci=ci, cc=cc: gla_chunk(ci * gla_per_ssd + cc))
            for g in range(SSD_GROUPS):
                tasks.append(lambda ci=ci, g=g: ssd_group(ci, g))
        return tasks

    def out_task(r0):
        return [lambda: out_rows(slice(r0, r0 + SUB_BLOCK))]

    def issue_alternating(a, b):
        order = sorted([((i + 0.5) / len(a), 0, i) for i in range(len(a))]
                       + [((i + 0.5) / len(b), 1, i) for i in range(len(b))])
        for _, which, i in order:
            (a, b)[which][i]()

    sub_starts = list(range(0, tb, SUB_BLOCK))
    for task in project_tasks(sub_starts[0]):
        task()
    pending_out = []
    for prev, cur in zip(sub_starts[:-1], sub_starts[1:]):
        issue_alternating(project_tasks(cur), pending_out + recur_tasks(prev))
        pending_out = out_task(prev)
    for task in pending_out + recur_tasks(sub_starts[-1]) + out_task(sub_starts[-1]):
        task()


def _const_spec(shape):
    zeros = (0,) * len(shape)
    return pl.BlockSpec(shape, lambda b, t: zeros, pipeline_mode=pl.Buffered(1))


def _layer(x, ada, vecs, wmain, wsm, w2, wout):
    bsz, seq, _ = x.shape
    tb = TIME_BLOCK
    consts = (ada, vecs, wmain, wsm, w2, wout)
    return pl.pallas_call(
        _layer_kernel,
        grid=(bsz, seq // tb),
        in_specs=[pl.BlockSpec((1, tb, D_MODEL), lambda b, t: (b, t, 0))]
        + [_const_spec(a.shape) for a in consts],
        out_specs=pl.BlockSpec((1, tb, D_MODEL), lambda b, t: (b, t, 0)),
        out_shape=jax.ShapeDtypeStruct(x.shape, x.dtype),
        scratch_shapes=[
            pltpu.VMEM((tb, D_MODEL), BF16),
            pltpu.VMEM((tb, LANES), F32),
            pltpu.VMEM((SUB_BLOCK + CONV_PAD, SSD_CONV_DIM), F32),
            pltpu.VMEM((tb, SSD_INNER), F32),
            pltpu.VMEM((tb, GLA_KEY), F32),
            pltpu.VMEM((tb, GLA_KEY), BF16),
            pltpu.VMEM((tb, GLA_KEY), BF16),
            pltpu.VMEM((tb, GLA_KEY), BF16),
            pltpu.VMEM((tb // GLA_CHUNK, GLA_KEY), F32),
            pltpu.VMEM((tb, GLA_VAL), BF16),
            pltpu.VMEM((tb, GLA_VAL), BF16),
            pltpu.VMEM((tb, SSD_INNER), BF16),
            pltpu.VMEM((tb, D_MIX), BF16),
            pltpu.VMEM((GLA_HEADS, GLA_DV, GLA_DK), F32),
            pltpu.VMEM((SSD_GROUPS, SSD_STATE, SSD_GROUP_WIDTH), F32),
            pltpu.VMEM((2 * LANES, GLA_KEY), BF16),
            pltpu.VMEM((LANES, GLA_KEY), BF16),
            pltpu.VMEM((2 * LANES, SSD_INNER), BF16),
            pltpu.VMEM((tb, SSD_INNER), BF16),
            pltpu.VMEM((tb, SSD_INNER), BF16),
            pltpu.VMEM((tb, 2 * SSD_GROUPS * SSD_STATE), BF16),
            pltpu.VMEM((GLA_HEADS, GLA_DV, GLA_DK), BF16),
            pltpu.VMEM((SSD_GROUPS, SSD_STATE, SSD_GROUP_WIDTH), BF16),
            pltpu.VMEM((GLA_HEADS, tb, GLA_CHUNK), BF16),
            pltpu.VMEM((tb, SSD_HEADS * SSD_CHUNK), BF16),
            pltpu.VMEM((tb, SSD_INNER), BF16),
            pltpu.VMEM((tb, SSD_INNER), F32),
            pltpu.VMEM((tb // SSD_CHUNK, SSD_INNER), F32),
        ],
        compiler_params=pltpu.CompilerParams(
            dimension_semantics=("arbitrary", "arbitrary"),
            vmem_limit_bytes=VMEM_LIMIT_BYTES),
        name="hybrid_layer",
    )(x, *consts)


def _prep_kernel(wt_ref, gate_ref, dt_ref, wmain_ref, wsm_ref):
    wmain_ref[...] = wt_ref[...].T.astype(BF16)

    @pl.when(pl.program_id(0) == 0)
    def _():
        narrow_t = jnp.concatenate(
            [gate_ref[...], dt_ref[...], jnp.zeros((LANES - DT_LANE - SSD_HEADS, D_MODEL), F32)], axis=0)
        hi, lo = _split(narrow_t.T)
        wsm_ref[:, 0:LANES] = hi
        wsm_ref[:, LANES:2 * LANES] = lo


def _prep_weights(w_t):
    def src_row(j):
        return pl.multiple_of(j * PROJ_TILE + jnp.where(j * PROJ_TILE >= IN_GATE, GLA_GATE_RANK, 0), SUBLANES)

    return pl.pallas_call(
        _prep_kernel,
        grid=(W_MAIN_COLS // PROJ_TILE,),
        in_specs=[pl.BlockSpec((pl.Element(PROJ_TILE), pl.Element(D_MODEL)), lambda j: (src_row(j), 0)),
                  pl.BlockSpec((pl.Element(GLA_GATE_RANK), pl.Element(D_MODEL)), lambda j: (IN_GATE, 0)),
                  pl.BlockSpec((pl.Element(SSD_HEADS), pl.Element(D_MODEL)), lambda j: (IN_DT, 0))],
        out_specs=[pl.BlockSpec((D_MODEL, PROJ_TILE), lambda j: (0, j)),
                   pl.BlockSpec((D_MODEL, 2 * LANES), lambda j: (0, 0))],
        out_shape=[jax.ShapeDtypeStruct((D_MODEL, W_MAIN_COLS), BF16),
                   jax.ShapeDtypeStruct((D_MODEL, 2 * LANES), BF16)],
        compiler_params=pltpu.CompilerParams(dimension_semantics=("arbitrary",)),
        name="prep_weights",
    )(w_t, w_t, w_t)


def _row(v, offset=0):
    tail = VEC_COLS - offset - v.shape[0]
    return [jnp.zeros((offset,), F32), v.astype(F32), jnp.zeros((tail,), F32)]


def kernel(x, c, ada_w, ada_b, pre_norm_w, w_in, gla_gate_w2, gla_gate_b, gla_norm_w, conv_w, conv_b,
           dt_bias, a_log, d_skip, ssd_norm_w, w_out, post_norm_w):
    bsz, seq, _ = x.shape
    assert seq % TIME_BLOCK == 0 and bsz <= SUBLANES
    depth = ada_w.shape[0]
    c_pad = jnp.pad(c, ((0, SUBLANES - bsz), (0, 0)))
    for i in range(depth):
        ada = _ada(c_pad, ada_w[i], ada_b[i][None, :])
        wmain, wsm2 = _prep_weights(w_in[i].T)
        rows = ([_row(pre_norm_w[i]), _row(gla_gate_b[i]), _row(gla_norm_w[i]), _row(conv_b[i])]
                + [_row(conv_w[i, kk]) for kk in range(SSD_CONV)]
                + [_row(dt_bias[i], DT_LANE), _row(a_log[i], DT_LANE),
                   _row(jnp.repeat(d_skip[i], SSD_HEAD_DIM)), _row(ssd_norm_w[i]), _row(post_norm_w[i])]
                + [[jnp.zeros(((VEC_ROWS - VEC_USED) * VEC_COLS,), F32)]])
        vecs = jnp.concatenate([piece for row in rows for piece in row]).reshape(VEC_ROWS, VEC_COLS)
        w2 = jnp.pad(gla_gate_w2[i], ((0, LANES - GLA_GATE_RANK), (0, 0)))
        x = _layer(x, ada, vecs, wmain, wsm2, w2, w_out[i].astype(BF16))
    return x
```

```python
import jax
import jax.numpy as jnp
from jax import lax
from jax.experimental import pallas as pl
from jax.experimental.pallas import tpu as pltpu

F32 = jnp.float32
BF16 = jnp.bfloat16

NORM_EPS = 1e-6
D_MODEL = 1024
GLA_HEADS = 4
GLA_DK = 128
GLA_DV = 256
GLA_KEY = GLA_HEADS * GLA_DK
GLA_VAL = GLA_HEADS * GLA_DV
GLA_GATE_RANK = 16
GLA_TAU = 16.0
GLA_CHUNK = 64
SSD_INNER = 1024
SSD_HEAD_DIM = 64
SSD_HEADS = 16
SSD_GROUPS = 2
SSD_HEADS_PER_GROUP = SSD_HEADS // SSD_GROUPS
SSD_GROUP_WIDTH = SSD_INNER // SSD_GROUPS
SSD_STATE = 128
SSD_CONV = 4
SSD_CHUNK = 128
SSD_CONV_DIM = SSD_INNER + 2 * SSD_GROUPS * SSD_STATE
D_MIX = GLA_VAL + SSD_INNER

LANES = 128
SUBLANES = 8

TIME_BLOCK = 1024
ROW_CHUNK = 128
PROJ_TILE = 512
SUB_BLOCK = 256
COL_QK = 0
COL_V = COL_QK + 2 * GLA_KEY
COL_G = COL_V + GLA_VAL
COL_Z = COL_G + GLA_VAL
COL_XBC = COL_Z + SSD_INNER
W_MAIN_COLS = COL_XBC + SSD_CONV_DIM
DT_LANE = GLA_GATE_RANK
assert DT_LANE + SSD_HEADS <= LANES
IN_GATE = 2 * GLA_KEY + 2 * GLA_VAL
IN_Z = IN_GATE + GLA_GATE_RANK
IN_DT = IN_Z + SSD_INNER + SSD_CONV_DIM
D_PROJ = IN_DT + SSD_HEADS
assert IN_GATE % PROJ_TILE == 0 and W_MAIN_COLS % PROJ_TILE == 0 and SSD_INNER % PROJ_TILE == 0
assert 2 * SSD_HEAD_DIM == LANES
MASKED_LOG = -1e30
(VEC_PRE_W, VEC_GATE_B, VEC_GLA_NORM, VEC_CONV_B, VEC_CONV_W, VEC_DT_BIAS, VEC_A_LOG,
 VEC_D_SKIP, VEC_SSD_NORM, VEC_POST_W) = (0, 1, 2, 3, 4, 4 + SSD_CONV, 5 + SSD_CONV, 6 + SSD_CONV,
                                          7 + SSD_CONV, 8 + SSD_CONV)
VEC_USED = VEC_POST_W + 1
VEC_ROWS = -(-VEC_USED // SUBLANES) * SUBLANES
VEC_COLS = SSD_CONV_DIM
CONV_PAD = SUBLANES
VMEM_LIMIT_BYTES = 62 * 1024 * 1024


def _dot(a, b):
    return jnp.dot(a, b, preferred_element_type=F32)


def _dot_nt(a, b):
    return lax.dot_general(a, b, (((1,), (1,)), ((), ())), preferred_element_type=F32)


def _dot_tn(a, b):
    return lax.dot_general(a, b, (((0,), (0,)), ((), ())), preferred_element_type=F32)


def _split(x):
    hi = x.astype(BF16)
    lo = (x - hi.astype(F32)).astype(BF16)
    return hi, lo


def _silu(x):
    hx = 0.5 * x
    return hx + hx * jnp.tanh(hx)


def _log1p_exp_neg_abs(x):
    return jnp.log(1.0 + jnp.exp(-jnp.abs(x)))


def _tril(n, dtype):
    r = lax.broadcasted_iota(jnp.int32, (n, n), 0)
    c = lax.broadcasted_iota(jnp.int32, (n, n), 1)
    return r >= c, (r >= c).astype(dtype)


def _ada_kernel(c_ref, w_ref, b_ref, o_ref):
    c = c_ref[...]
    ca_hi, ca_lo = _split(_silu(c))
    w_hi, w_lo = _split(w_ref[...])
    o_ref[...] = _dot(ca_hi, w_hi) + _dot(ca_hi, w_lo) + _dot(ca_lo, w_hi) + b_ref[...]


def _ada(c_pad, ada_w, ada_b):
    rows = c_pad.shape[0]
    return pl.pallas_call(
        _ada_kernel,
        grid=(3,),
        in_specs=[
            pl.BlockSpec((rows, D_MODEL), lambda j: (0, 0)),
            pl.BlockSpec((D_MODEL, D_MODEL), lambda j: (0, j)),
            pl.BlockSpec((1, D_MODEL), lambda j: (0, j)),
        ],
        out_specs=pl.BlockSpec((rows, D_MODEL), lambda j: (0, j)),
        out_shape=jax.ShapeDtypeStruct((rows, 3 * D_MODEL), F32),
        compiler_params=pltpu.CompilerParams(dimension_semantics=("arbitrary",)),
        name="adaln",
    )(c_pad, ada_w, ada_b)


def _layer_kernel(x_ref, ada_ref, vec_ref, wmain_ref, wsm_ref, w2_ref, wout_ref,
                  out_ref,
                  h_s, sm_s, xbc_s, u_s, b_s, qd_s, kd_s, ke_s, dec_s, v_s, gact_s, zact_s, mix_s,
                  gstate_s, sstate_s, w2a_s, w2b_s, expand_s, xlo_s, xhi_s, bc_s,
                  gstate_b_s, sstate_b_s, att_s, w_s, xss_s, decin_s, cdec_s):
    tb = TIME_BLOCK
    t = pl.program_id(1)

    @pl.when(t == 0)
    def _():
        gstate_s[...] = jnp.zeros_like(gstate_s)
        sstate_s[...] = jnp.zeros_like(sstate_s)
        gstate_b_s[...] = jnp.zeros_like(gstate_b_s)
        sstate_b_s[...] = jnp.zeros_like(sstate_b_s)
        xbc_s[0:CONV_PAD, :] = jnp.zeros((CONV_PAD, SSD_CONV_DIM), F32)
        w2_hi, w2_lo = _split(w2_ref[...])
        w2a_s[0:LANES, :] = w2_hi
        w2a_s[LANES:2 * LANES, :] = w2_hi
        w2b_s[...] = w2_lo
        head_row = lax.broadcasted_iota(jnp.int32, (2 * LANES, SSD_INNER), 0) % LANES
        head_of_col = lax.broadcasted_iota(jnp.int32, (2 * LANES, SSD_INNER), 1) // SSD_HEAD_DIM + DT_LANE
        expand_s[...] = (head_row == head_of_col).astype(BF16)

    def vec_view(row, width, rows=1):
        return vec_ref.at[row:row + rows, 0:width]

    prew_ref = vec_view(VEC_PRE_W, D_MODEL)
    gb_ref = vec_view(VEC_GATE_B, GLA_KEY)
    gnw_ref = vec_view(VEC_GLA_NORM, GLA_DV)
    cb_ref = vec_view(VEC_CONV_B, SSD_CONV_DIM)
    cw_ref = vec_view(VEC_CONV_W, SSD_CONV_DIM, SSD_CONV)
    dtb_ref = vec_view(VEC_DT_BIAS, LANES)
    alog_ref = vec_view(VEC_A_LOG, LANES)
    dskip_ref = vec_view(VEC_D_SKIP, SSD_INNER)
    snw_ref = vec_view(VEC_SSD_NORM, SSD_INNER)
    postw_ref = vec_view(VEC_POST_W, D_MODEL)

    ada = ada_ref[pl.ds(pl.program_id(0), 1), :]
    shift = ada[:, 0:D_MODEL]
    scale1 = 1.0 + ada[:, D_MODEL:2 * D_MODEL]
    gate = ada[:, 2 * D_MODEL:3 * D_MODEL]
    causal_g, tril_g = _tril(GLA_CHUNK, BF16)
    tril_g2 = jnp.concatenate([tril_g, tril_g], axis=1)

    def project_tasks(r0):
        sub = slice(r0, r0 + SUB_BLOCK)
        row_chunks = [slice(r, r + ROW_CHUNK) for r in range(r0, r0 + SUB_BLOCK, ROW_CHUNK)]
        tasks = []

        def proj(col, width):
            return _dot(h_s[sub, :], wmain_ref[:, col:col + width])

        def prenorm(rs):
            x = x_ref[0, rs, :]
            ms = jnp.mean(x * x, axis=-1, keepdims=True)
            h = (x * lax.rsqrt(ms + NORM_EPS)) * prew_ref[...] * scale1 + shift
            h_hi = h.astype(BF16)
            h_lo = (h - h_hi.astype(F32)).astype(BF16)
            h_s[rs, :] = h_hi
            both = _dot(h_hi, wsm_ref[...])
            sm_s[rs, :] = both[:, 0:LANES] + both[:, LANES:2 * LANES] + _dot(h_lo, wsm_ref[:, 0:LANES])

        def forget_gate(rs):
            sm_hi, sm_lo = _split(sm_s[rs, :])
            sm_both = jnp.concatenate([sm_hi, sm_lo], axis=1)
            gl = _dot(sm_both, w2a_s[...]) + _dot(sm_hi, w2b_s[...]) + gb_ref[...]
            la = (jnp.minimum(gl, 0.0) - _log1p_exp_neg_abs(gl)) * (1.0 / GLA_TAU)
            for cc in range(ROW_CHUNK // GLA_CHUNK):
                la_hi, la_lo = _split(la[cc * GLA_CHUNK:(cc + 1) * GLA_CHUNK, :])
                b = _dot(tril_g2, jnp.concatenate([la_hi, la_lo], axis=0))
                c0 = rs.start + cc * GLA_CHUNK
                b_s[c0:c0 + GLA_CHUNK, :] = b
                ci = c0 // GLA_CHUNK
                dec_s[ci:ci + 1, :] = jnp.exp(b[GLA_CHUNK - 1:GLA_CHUNK, :])

        def xbc_tile(cs):
            xbc_s[CONV_PAD:CONV_PAD + SUB_BLOCK, cs] = proj(COL_XBC + cs.start, PROJ_TILE)

        def conv(rs, cs):
            a = xbc_s[rs.start - r0:rs.start - r0 + ROW_CHUNK + CONV_PAD, cs]
            acc = cw_ref[0:1, cs] * a
            for kk in range(1, SSD_CONV):
                acc = pltpu.roll(acc, 1, 0) + cw_ref[kk:kk + 1, cs] * a
            u = _silu(acc[CONV_PAD:, :] + cb_ref[:, cs])
            if cs.start < SSD_INNER:
                u_s[rs, cs] = u
                first_head = lax.broadcasted_iota(jnp.int32, u.shape, 1) % LANES < SSD_HEAD_DIM
                xlo_s[rs, cs] = jnp.where(first_head, u, 0.0).astype(BF16)
                xhi_s[rs, cs] = jnp.where(first_head, 0.0, u).astype(BF16)
            else:
                bc_s[rs, cs.start - SSD_INNER:cs.stop - SSD_INNER] = u.astype(BF16)

        def q_tile():
            q = proj(COL_QK, GLA_KEY)
            qd_s[sub, :] = (q * (GLA_DK ** -0.5) * jnp.exp(b_s[sub, :])).astype(BF16)

        def k_tile():
            k = proj(COL_QK + GLA_KEY, GLA_KEY)
            for cc in range(SUB_BLOCK // GLA_CHUNK):
                rows = slice(r0 + cc * GLA_CHUNK, r0 + (cc + 1) * GLA_CHUNK)
                b = b_s[rows, :]
                k_c = k[cc * GLA_CHUNK:(cc + 1) * GLA_CHUNK, :]
                kd_s[rows, :] = (k_c * jnp.exp(-b)).astype(BF16)
                ke_s[rows, :] = (k_c * jnp.exp(b[GLA_CHUNK - 1:GLA_CHUNK, :] - b)).astype(BF16)

        def act_tile(dst_s, col, cs):
            dst_s[sub, cs] = _silu(proj(col + cs.start, PROJ_TILE).astype(BF16))

        def v_tile(cs):
            v_s[sub, cs] = proj(COL_V + cs.start, PROJ_TILE).astype(BF16)

        for rs in row_chunks:
            tasks.append(lambda rs=rs: prenorm(rs))
        for rs in row_chunks:
            tasks.append(lambda rs=rs: forget_gate(rs))
        for col in range(0, SSD_CONV_DIM, PROJ_TILE):
            cs = slice(col, col + PROJ_TILE)
            tasks.append(lambda cs=cs: xbc_tile(cs))
            for rs in row_chunks:
                tasks.append(lambda rs=rs, cs=cs: conv(rs, cs))
        def keep_conv_tail():
            xbc_s[0:CONV_PAD, :] = xbc_s[SUB_BLOCK:SUB_BLOCK + CONV_PAD, :]

        tasks.append(keep_conv_tail)
        ssd_chunks = range(r0 // SSD_CHUNK, (r0 + SUB_BLOCK) // SSD_CHUNK)
        ssd_shared = {ci: {} for ci in ssd_chunks}
        for ci in ssd_chunks:
            tasks.append(lambda ci=ci: ssd_decays(ci, ssd_shared[ci]))
        tasks.append(q_tile)
        tasks.append(k_tile)
        for ci in ssd_chunks:
            for g in range(SSD_GROUPS):
                tasks.append(lambda ci=ci, g=g: ssd_weights(ci, g, ssd_shared[ci]))
        for ci in range(r0 // GLA_CHUNK, (r0 + SUB_BLOCK) // GLA_CHUNK):
            tasks.append(lambda ci=ci: gla_scores(ci))
        for col in range(0, GLA_VAL, PROJ_TILE):
            cs = slice(col, col + PROJ_TILE)
            tasks.append(lambda cs=cs: act_tile(gact_s, COL_G, cs))
        for col in range(0, SSD_INNER, PROJ_TILE):
            cs = slice(col, col + PROJ_TILE)
            tasks.append(lambda cs=cs: act_tile(zact_s, COL_Z, cs))
        for col in range(0, GLA_VAL, PROJ_TILE):
            cs = slice(col, col + PROJ_TILE)
            tasks.append(lambda cs=cs: v_tile(cs))
        return tasks

    def gla_scores(ci):
        rows = slice(ci * GLA_CHUNK, (ci + 1) * GLA_CHUNK)
        for hh in range(GLA_HEADS):
            ks = slice(hh * GLA_DK, (hh + 1) * GLA_DK)
            att = _dot_nt(qd_s[rows, ks], kd_s[rows, ks])
            att_s[hh, rows, :] = jnp.where(causal_g, att, 0.0).astype(BF16)

    def gla_chunk(ci):
        rows = slice(ci * GLA_CHUNK, (ci + 1) * GLA_CHUNK)
        decay = dec_s[ci:ci + 1, :]
        for hh in range(GLA_HEADS):
            ks = slice(hh * GLA_DK, (hh + 1) * GLA_DK)
            vs = slice(hh * GLA_DV, (hh + 1) * GLA_DV)
            v_h = v_s[rows, vs]
            o = _dot(att_s[hh, rows, :], v_h) + _dot_nt(qd_s[rows, ks], gstate_b_s[hh])
            state_t = gstate_s[hh] * decay[:, ks] + _dot_tn(v_h, ke_s[rows, ks])
            gstate_s[hh] = state_t
            gstate_b_s[hh] = state_t.astype(BF16)
            o = (o * lax.rsqrt(jnp.mean(o * o, axis=-1, keepdims=True) + NORM_EPS)) * gnw_ref[...]
            mix_s[rows, vs] = (o * gact_s[rows, vs].astype(F32)).astype(BF16)

    causal_s, tril_s = _tril(SSD_CHUNK, BF16)
    tril_s2 = jnp.concatenate([tril_s, tril_s], axis=1)
    a_neg = -jnp.exp(alog_ref[...])

    def expand_heads(v):
        hi, lo = _split(v)
        return _dot(jnp.concatenate([hi, lo], axis=1), expand_s[...])

    def ssd_decays(ci, shared):
        rows = slice(ci * SSD_CHUNK, (ci + 1) * SSD_CHUNK)
        dt_in = sm_s[rows, :] + dtb_ref[...]
        dt = jnp.maximum(dt_in, 0.0) + _log1p_exp_neg_abs(dt_in)
        dta_hi, dta_lo = _split(dt * a_neg)
        a_cum = _dot(tril_s2, jnp.concatenate([dta_hi, dta_lo], axis=0))
        a_last = a_cum[SSD_CHUNK - 1:SSD_CHUNK, :]
        decay_in_x = expand_heads(jnp.exp(a_cum))
        decin_s[rows, :] = decay_in_x
        cdec_s[ci:ci + 1, :] = decay_in_x[SSD_CHUNK - 1:SSD_CHUNK, :]
        xss_s[rows, :] = (u_s[rows, :] * expand_heads(jnp.exp(a_last - a_cum) * dt)).astype(BF16)
        shared.update(a_cum=a_cum, src_t=(a_cum - jnp.log(dt)).T)

    def ssd_weights(ci, g, shared):
        rows = slice(ci * SSD_CHUNK, (ci + 1) * SSD_CHUNK)
        b_g = bc_s[rows, g * SSD_STATE:(g + 1) * SSD_STATE]
        c_g = bc_s[rows, (SSD_GROUPS + g) * SSD_STATE:(SSD_GROUPS + g + 1) * SSD_STATE]
        a_cum, src_t = shared["a_cum"], shared["src_t"]
        cb = _dot_nt(c_g, b_g)
        for hl in range(SSD_HEADS_PER_GROUP):
            head = g * SSD_HEADS_PER_GROUP + hl
            lane = DT_LANE + head
            seg = jnp.where(causal_s, a_cum[:, lane:lane + 1] - src_t[lane:lane + 1, :], MASKED_LOG)
            w_s[rows, head * SSD_CHUNK:(head + 1) * SSD_CHUNK] = (jnp.exp(seg) * cb).astype(BF16)

    def ssd_group(ci, g):
        rows = slice(ci * SSD_CHUNK, (ci + 1) * SSD_CHUNK)
        gs = slice(g * SSD_GROUP_WIDTH, (g + 1) * SSD_GROUP_WIDTH)
        b_g = bc_s[rows, g * SSD_STATE:(g + 1) * SSD_STATE]
        c_g = bc_s[rows, (SSD_GROUPS + g) * SSD_STATE:(SSD_GROUPS + g + 1) * SSD_STATE]
        y_g = _dot(c_g, sstate_b_s[g]) * decin_s[rows, gs]
        state = sstate_s[g] * cdec_s[ci:ci + 1, gs] + _dot_tn(b_g, xss_s[rows, gs])
        sstate_s[g] = state
        sstate_b_s[g] = state.astype(BF16)
        y_tiles = []
        for pair in range(SSD_HEADS_PER_GROUP // 2):
            ts = slice(gs.start + pair * LANES, gs.start + (pair + 1) * LANES)
            head = g * SSD_HEADS_PER_GROUP + 2 * pair
            x_pair = jnp.concatenate([xlo_s[rows, ts], xhi_s[rows, ts]], axis=0)
            y_tiles.append(_dot(w_s[rows, head * SSD_CHUNK:(head + 2) * SSD_CHUNK], x_pair))
        y_g = y_g + jnp.concatenate(y_tiles, axis=1) + dskip_ref[:, gs] * u_s[rows, gs]
        y_g = y_g * zact_s[rows, gs].astype(F32)
        y_g = (y_g * lax.rsqrt(jnp.mean(y_g * y_g, axis=-1, keepdims=True) + NORM_EPS)) * snw_ref[:, gs]
        mix_s[rows, GLA_VAL + g * SSD_GROUP_WIDTH:GLA_VAL + (g + 1) * SSD_GROUP_WIDTH] = y_g.astype(BF16)

    def out_rows(rs):
        mixed = _dot(mix_s[rs, :], wout_ref[...])
        mixed = (mixed * lax.rsqrt(jnp.mean(mixed * mixed, axis=-1, keepdims=True) + NORM_EPS)) * postw_ref[...]
        out_ref[0, rs, :] = x_ref[0, rs, :] + gate * mixed

    def recur_tasks(r0):
        tasks = []
        gla_per_ssd = SSD_CHUNK // GLA_CHUNK
        for ci in range(r0 // SSD_CHUNK, (r0 + SUB_BLOCK) // SSD_CHUNK):
            for cc in range(gla_per_ssd):
                tasks.append(lambda ci=ci, cc=cc: gla_chunk(ci * gla_per_ssd + cc))
            for g in range(SSD_GROUPS):
                tasks.append(lambda ci=ci, g=g: ssd_group(ci, g))
        tasks.append(lambda: out_rows(slice(r0, r0 + SUB_BLOCK)))
        return tasks

    def issue_alternating(a, b):
        order = sorted([((i + 1.0) / len(a), 0, i) for i in range(len(a))]
                       + [((i + 0.0) / len(b), 1, i) for i in range(len(b))])
        for _, which, i in order:
            (a, b)[which][i]()

    sub_starts = list(range(0, tb, SUB_BLOCK))
    for task in project_tasks(sub_starts[0]):
        task()
    for prev, cur in zip(sub_starts[:-1], sub_starts[1:]):
        issue_alternating(project_tasks(cur), recur_tasks(prev))
    for task in recur_tasks(sub_starts[-1]):
        task()


def _const_spec(shape):
    zeros = (0,) * len(shape)
    return pl.BlockSpec(shape, lambda b, t: zeros, pipeline_mode=pl.Buffered(1))


def _layer(x, ada, vecs, wmain, wsm, w2, wout):
    bsz, seq, _ = x.shape
    tb = TIME_BLOCK
    consts = (ada, vecs, wmain, wsm, w2, wout)
    return pl.pallas_call(
        _layer_kernel,
        grid=(bsz, seq // tb),
        in_specs=[pl.BlockSpec((1, tb, D_MODEL), lambda b, t: (b, t, 0))]
        + [_const_spec(a.shape) for a in consts],
        out_specs=pl.BlockSpec((1, tb, D_MODEL), lambda b, t: (b, t, 0)),
        out_shape=jax.ShapeDtypeStruct(x.shape, x.dtype),
        scratch_shapes=[
            pltpu.VMEM((tb, D_MODEL), BF16),
            pltpu.VMEM((tb, LANES), F32),
            pltpu.VMEM((SUB_BLOCK + CONV_PAD, SSD_CONV_DIM), F32),
            pltpu.VMEM((tb, SSD_INNER), F32),
            pltpu.VMEM((tb, GLA_KEY), F32),
            pltpu.VMEM((tb, GLA_KEY), BF16),
            pltpu.VMEM((tb, GLA_KEY), BF16),
            pltpu.VMEM((tb, GLA_KEY), BF16),
            pltpu.VMEM((tb // GLA_CHUNK, GLA_KEY), F32),
            pltpu.VMEM((tb, GLA_VAL), BF16),
            pltpu.VMEM((tb, GLA_VAL), BF16),
            pltpu.VMEM((tb, SSD_INNER), BF16),
            pltpu.VMEM((tb, D_MIX), BF16),
            pltpu.VMEM((GLA_HEADS, GLA_DV, GLA_DK), F32),
            pltpu.VMEM((SSD_GROUPS, SSD_STATE, SSD_GROUP_WIDTH), F32),
            pltpu.VMEM((2 * LANES, GLA_KEY), BF16),
            pltpu.VMEM((LANES, GLA_KEY), BF16),
            pltpu.VMEM((2 * LANES, SSD_INNER), BF16),
            pltpu.VMEM((tb, SSD_INNER), BF16),
            pltpu.VMEM((tb, SSD_INNER), BF16),
            pltpu.VMEM((tb, 2 * SSD_GROUPS * SSD_STATE), BF16),
            pltpu.VMEM((GLA_HEADS, GLA_DV, GLA_DK), BF16),
            pltpu.VMEM((SSD_GROUPS, SSD_STATE, SSD_GROUP_WIDTH), BF16),
            pltpu.VMEM((GLA_HEADS, tb, GLA_CHUNK), BF16),
            pltpu.VMEM((tb, SSD_HEADS * SSD_CHUNK), BF16),
            pltpu.VMEM((tb, SSD_INNER), BF16),
            pltpu.VMEM((tb, SSD_INNER), F32),
            pltpu.VMEM((tb // SSD_CHUNK, SSD_INNER), F32),
        ],
        compiler_params=pltpu.CompilerParams(
            dimension_semantics=("arbitrary", "arbitrary"),
            vmem_limit_bytes=VMEM_LIMIT_BYTES),
        name="hybrid_layer",
    )(x, *consts)


def _prep_kernel(wt_ref, gate_ref, dt_ref, wmain_ref, wsm_ref):
    wmain_ref[...] = wt_ref[...].T.astype(BF16)

    @pl.when(pl.program_id(0) == 0)
    def _():
        narrow_t = jnp.concatenate(
            [gate_ref[...], dt_ref[...], jnp.zeros((LANES - DT_LANE - SSD_HEADS, D_MODEL), F32)], axis=0)
        hi, lo = _split(narrow_t.T)
        wsm_ref[:, 0:LANES] = hi
        wsm_ref[:, LANES:2 * LANES] = lo


def _prep_weights(w_t):
    def src_row(j):
        return pl.multiple_of(j * PROJ_TILE + jnp.where(j * PROJ_TILE >= IN_GATE, GLA_GATE_RANK, 0), SUBLANES)

    return pl.pallas_call(
        _prep_kernel,
        grid=(W_MAIN_COLS // PROJ_TILE,),
        in_specs=[pl.BlockSpec((pl.Element(PROJ_TILE), pl.Element(D_MODEL)), lambda j: (src_row(j), 0)),
                  pl.BlockSpec((pl.Element(GLA_GATE_RANK), pl.Element(D_MODEL)), lambda j: (IN_GATE, 0)),
                  pl.BlockSpec((pl.Element(SSD_HEADS), pl.Element(D_MODEL)), lambda j: (IN_DT, 0))],
        out_specs=[pl.BlockSpec((D_MODEL, PROJ_TILE), lambda j: (0, j)),
                   pl.BlockSpec((D_MODEL, 2 * LANES), lambda j: (0, 0))],
        out_shape=[jax.ShapeDtypeStruct((D_MODEL, W_MAIN_COLS), BF16),
                   jax.ShapeDtypeStruct((D_MODEL, 2 * LANES), BF16)],
        compiler_params=pltpu.CompilerParams(dimension_semantics=("arbitrary",)),
        name="prep_weights",
    )(w_t, w_t, w_t)


def _row(v, offset=0):
    tail = VEC_COLS - offset - v.shape[0]
    return [jnp.zeros((offset,), F32), v.astype(F32), jnp.zeros((tail,), F32)]


def kernel(x, c, ada_w, ada_b, pre_norm_w, w_in, gla_gate_w2, gla_gate_b, gla_norm_w, conv_w, conv_b,
           dt_bias, a_log, d_skip, ssd_norm_w, w_out, post_norm_w):
    bsz, seq, _ = x.shape
    assert seq % TIME_BLOCK == 0 and bsz <= SUBLANES
    depth = ada_w.shape[0]
    c_pad = jnp.pad(c, ((0, SUBLANES - bsz), (0, 0)))
    for i in range(depth):
        ada = _ada(c_pad, ada_w[i], ada_b[i][None, :])
        wmain, wsm2 = _prep_weights(w_in[i].T)
        rows = ([_row(pre_norm_w[i]), _row(gla_gate_b[i]), _row(gla_norm_w[i]), _row(conv_b[i])]
                + [_row(conv_w[i, kk]) for kk in range(SSD_CONV)]
                + [_row(dt_bias[i], DT_LANE), _row(a_log[i], DT_LANE),
                   _row(jnp.repeat(d_skip[i], SSD_HEAD_DIM)), _row(ssd_norm_w[i]), _row(post_norm_w[i])]
                + [[jnp.zeros(((VEC_ROWS - VEC_USED) * VEC_COLS,), F32)]])
        vecs = jnp.concatenate([piece for row in rows for piece in row]).reshape(VEC_ROWS, VEC_COLS)
        w2 = jnp.pad(gla_gate_w2[i], ((0, LANES - GLA_GATE_RANK), (0, 0)))
        x = _layer(x, ada, vecs, wmain, wsm2, w2, w_out[i].astype(BF16))
    return x
```

```python
import jax
import jax.numpy as jnp
from jax import lax
from jax.experimental import pallas as pl
from jax.experimental.pallas import tpu as pltpu

F32 = jnp.float32
BF16 = jnp.bfloat16

NORM_EPS = 1e-6
D_MODEL = 1024
GLA_HEADS = 4
GLA_DK = 128
GLA_DV = 256
GLA_KEY = GLA_HEADS * GLA_DK
GLA_VAL = GLA_HEADS * GLA_DV
GLA_GATE_RANK = 16
GLA_TAU = 16.0
GLA_CHUNK = 64
SSD_INNER = 1024
SSD_HEAD_DIM = 64
SSD_HEADS = 16
SSD_GROUPS = 2
SSD_HEADS_PER_GROUP = SSD_HEADS // SSD_GROUPS
SSD_GROUP_WIDTH = SSD_INNER // SSD_GROUPS
SSD_STATE = 128
SSD_CONV = 4
SSD_CHUNK = 128
SSD_CONV_DIM = SSD_INNER + 2 * SSD_GROUPS * SSD_STATE
D_MIX = GLA_VAL + SSD_INNER

LANES = 128
SUBLANES = 8

TIME_BLOCK = 1024
ROW_CHUNK = 128
PROJ_TILE = 512
SUB_BLOCK = 256
COL_QK = 0
COL_V = COL_QK + 2 * GLA_KEY
COL_G = COL_V + GLA_VAL
COL_Z = COL_G + GLA_VAL
COL_XBC = COL_Z + SSD_INNER
W_MAIN_COLS = COL_XBC + SSD_CONV_DIM
DT_LANE = GLA_GATE_RANK
assert DT_LANE + SSD_HEADS <= LANES
IN_GATE = 2 * GLA_KEY + 2 * GLA_VAL
IN_Z = IN_GATE + GLA_GATE_RANK
IN_DT = IN_Z + SSD_INNER + SSD_CONV_DIM
D_PROJ = IN_DT + SSD_HEADS
assert IN_GATE % PROJ_TILE == 0 and W_MAIN_COLS % PROJ_TILE == 0 and SSD_INNER % PROJ_TILE == 0
assert 2 * SSD_HEAD_DIM == LANES
MASKED_LOG = -1e30
(VEC_PRE_W, VEC_GATE_B, VEC_GLA_NORM, VEC_CONV_B, VEC_CONV_W, VEC_DT_BIAS, VEC_A_LOG,
 VEC_D_SKIP, VEC_SSD_NORM, VEC_POST_W) = (0, 1, 2, 3, 4, 4 + SSD_CONV, 5 + SSD_CONV, 6 + SSD_CONV,
                                          7 + SSD_CONV, 8 + SSD_CONV)
VEC_USED = VEC_POST_W + 1
VEC_ROWS = -(-VEC_USED // SUBLANES) * SUBLANES
VEC_COLS = SSD_CONV_DIM
CONV_PAD = SUBLANES
VMEM_LIMIT_BYTES = 62 * 1024 * 1024


def _dot(a, b):
    return jnp.dot(a, b, preferred_element_type=F32)


def _dot_nt(a, b):
    return lax.dot_general(a, b, (((1,), (1,)), ((), ())), preferred_element_type=F32)


def _dot_tn(a, b):
    return lax.dot_general(a, b, (((0,), (0,)), ((), ())), preferred_element_type=F32)


def _split(x):
    hi = x.astype(BF16)
    lo = (x - hi.astype(F32)).astype(BF16)
    return hi, lo


def _silu(x):
    hx = 0.5 * x
    return hx + hx * jnp.tanh(hx)


def _log1p_exp_neg_abs(x):
    return jnp.log(1.0 + jnp.exp(-jnp.abs(x)))


def _tril(n, dtype):
    r = lax.broadcasted_iota(jnp.int32, (n, n), 0)
    c = lax.broadcasted_iota(jnp.int32, (n, n), 1)
    return r >= c, (r >= c).astype(dtype)


def _ada_kernel(c_ref, w_ref, b_ref, o_ref):
    c = c_ref[...]
    ca_hi, ca_lo = _split(_silu(c))
    w_hi, w_lo = _split(w_ref[...])
    o_ref[...] = _dot(ca_hi, w_hi) + _dot(ca_hi, w_lo) + _dot(ca_lo, w_hi) + b_ref[...]


def _ada(c_pad, ada_w, ada_b):
    rows = c_pad.shape[0]
    return pl.pallas_call(
        _ada_kernel,
        grid=(3,),
        in_specs=[
            pl.BlockSpec((rows, D_MODEL), lambda j: (0, 0)),
            pl.BlockSpec((D_MODEL, D_MODEL), lambda j: (0, j)),
            pl.BlockSpec((1, D_MODEL), lambda j: (0, j)),
        ],
        out_specs=pl.BlockSpec((rows, D_MODEL), lambda j: (0, j)),
        out_shape=jax.ShapeDtypeStruct((rows, 3 * D_MODEL), F32),
        compiler_params=pltpu.CompilerParams(dimension_semantics=("arbitrary",)),
        name="adaln",
    )(c_pad, ada_w, ada_b)


def _layer_kernel(x_ref, ada_ref, vec_ref, wmain_ref, wsm_ref, w2_ref, wout_ref,
                  out_ref,
                  h_s, sm_s, xbc_s, u_s, b_s, qd_s, kd_s, ke_s, dec_s, v_s, gact_s, zact_s, mix_s,
                  gstate_s, sstate_s, w2a_s, w2b_s, expand_s, xlo_s, xhi_s, bc_s,
                  gstate_b_s, sstate_b_s, att_s, w_s, xss_s, decin_s, cdec_s):
    tb = TIME_BLOCK
    t = pl.program_id(1)

    @pl.when(t == 0)
    def _():
        gstate_s[...] = jnp.zeros_like(gstate_s)
        sstate_s[...] = jnp.zeros_like(sstate_s)
        gstate_b_s[...] = jnp.zeros_like(gstate_b_s)
        sstate_b_s[...] = jnp.zeros_like(sstate_b_s)
        xbc_s[0:CONV_PAD, :] = jnp.zeros((CONV_PAD, SSD_CONV_DIM), F32)
        w2_hi, w2_lo = _split(w2_ref[...])
        w2a_s[0:LANES, :] = w2_hi
        w2a_s[LANES:2 * LANES, :] = w2_hi
        w2b_s[...] = w2_lo
        head_row = lax.broadcasted_iota(jnp.int32, (2 * LANES, SSD_INNER), 0) % LANES
        head_of_col = lax.broadcasted_iota(jnp.int32, (2 * LANES, SSD_INNER), 1) // SSD_HEAD_DIM + DT_LANE
        expand_s[...] = (head_row == head_of_col).astype(BF16)

    def vec_view(row, width, rows=1):
        return vec_ref.at[row:row + rows, 0:width]

    prew_ref = vec_view(VEC_PRE_W, D_MODEL)
    gb_ref = vec_view(VEC_GATE_B, GLA_KEY)
    gnw_ref = vec_view(VEC_GLA_NORM, GLA_DV)
    cb_ref = vec_view(VEC_CONV_B, SSD_CONV_DIM)
    cw_ref = vec_view(VEC_CONV_W, SSD_CONV_DIM, SSD_CONV)
    dtb_ref = vec_view(VEC_DT_BIAS, LANES)
    alog_ref = vec_view(VEC_A_LOG, LANES)
    dskip_ref = vec_view(VEC_D_SKIP, SSD_INNER)
    snw_ref = vec_view(VEC_SSD_NORM, SSD_INNER)
    postw_ref = vec_view(VEC_POST_W, D_MODEL)

    ada = ada_ref[pl.ds(pl.program_id(0), 1), :]
    shift = ada[:, 0:D_MODEL]
    scale1 = 1.0 + ada[:, D_MODEL:2 * D_MODEL]
    gate = ada[:, 2 * D_MODEL:3 * D_MODEL]
    causal_g, tril_g = _tril(GLA_CHUNK, BF16)
    tril_g2 = jnp.concatenate([tril_g, tril_g], axis=1)

    def project_tasks(r0):
        sub = slice(r0, r0 + SUB_BLOCK)
        row_chunks = [slice(r, r + ROW_CHUNK) for r in range(r0, r0 + SUB_BLOCK, ROW_CHUNK)]
        tasks = []

        def proj(col, width):
            return _dot(h_s[sub, :], wmain_ref[:, col:col + width])

        def prenorm(rs):
            x = x_ref[0, rs, :]
            ms = jnp.mean(x * x, axis=-1, keepdims=True)
            h = (x * lax.rsqrt(ms + NORM_EPS)) * prew_ref[...] * scale1 + shift
            h_hi = h.astype(BF16)
            h_lo = (h - h_hi.astype(F32)).astype(BF16)
            h_s[rs, :] = h_hi
            both = _dot(h_hi, wsm_ref[...])
            sm_s[rs, :] = both[:, 0:LANES] + both[:, LANES:2 * LANES] + _dot(h_lo, wsm_ref[:, 0:LANES])

        def forget_gate(rs):
            sm_hi, sm_lo = _split(sm_s[rs, :])
            sm_both = jnp.concatenate([sm_hi, sm_lo], axis=1)
            gl = _dot(sm_both, w2a_s[...]) + _dot(sm_hi, w2b_s[...]) + gb_ref[...]
            la = (jnp.minimum(gl, 0.0) - _log1p_exp_neg_abs(gl)) * (1.0 / GLA_TAU)
            for cc in range(ROW_CHUNK // GLA_CHUNK):
                la_hi, la_lo = _split(la[cc * GLA_CHUNK:(cc + 1) * GLA_CHUNK, :])
                b = _dot(tril_g2, jnp.concatenate([la_hi, la_lo], axis=0))
                c0 = rs.start + cc * GLA_CHUNK
                b_s[c0:c0 + GLA_CHUNK, :] = b
                ci = c0 // GLA_CHUNK
                dec_s[ci:ci + 1, :] = jnp.exp(b[GLA_CHUNK - 1:GLA_CHUNK, :])

        def xbc_tile(cs):
            xbc_s[CONV_PAD:CONV_PAD + SUB_BLOCK, cs] = proj(COL_XBC + cs.start, PROJ_TILE)

        def conv(rs, cs):
            a = xbc_s[rs.start - r0:rs.start - r0 + ROW_CHUNK + CONV_PAD, cs]
            acc = cw_ref[0:1, cs] * a
            for kk in range(1, SSD_CONV):
                acc = pltpu.roll(acc, 1, 0) + cw_ref[kk:kk + 1, cs] * a
            u = _silu(acc[CONV_PAD:, :] + cb_ref[:, cs])
            if cs.start < SSD_INNER:
                u_s[rs, cs] = u
                first_head = lax.broadcasted_iota(jnp.int32, u.shape, 1) % LANES < SSD_HEAD_DIM
                xlo_s[rs, cs] = jnp.where(first_head, u, 0.0).astype(BF16)
                xhi_s[rs, cs] = jnp.where(first_head, 0.0, u).astype(BF16)
            else:
                bc_s[rs, cs.start - SSD_INNER:cs.stop - SSD_INNER] = u.astype(BF16)

        def q_tile():
            q = proj(COL_QK, GLA_KEY)
            qd_s[sub, :] = (q * (GLA_DK ** -0.5) * jnp.exp(b_s[sub, :])).astype(BF16)

        def k_tile():
            k = proj(COL_QK + GLA_KEY, GLA_KEY)
            for cc in range(SUB_BLOCK // GLA_CHUNK):
                rows = slice(r0 + cc * GLA_CHUNK, r0 + (cc + 1) * GLA_CHUNK)
                b = b_s[rows, :]
                k_c = k[cc * GLA_CHUNK:(cc + 1) * GLA_CHUNK, :]
                kd_s[rows, :] = (k_c * jnp.exp(-b)).astype(BF16)
                ke_s[rows, :] = (k_c * jnp.exp(b[GLA_CHUNK - 1:GLA_CHUNK, :] - b)).astype(BF16)

        def act_tile(dst_s, col, cs):
            dst_s[sub, cs] = _silu(proj(col + cs.start, PROJ_TILE).astype(BF16))

        def v_tile(cs):
            v_s[sub, cs] = proj(COL_V + cs.start, PROJ_TILE).astype(BF16)

        for rs in row_chunks:
            tasks.append(lambda rs=rs: prenorm(rs))
        for rs in row_chunks:
            tasks.append(lambda rs=rs: forget_gate(rs))
        for col in range(0, SSD_CONV_DIM, PROJ_TILE):
            cs = slice(col, col + PROJ_TILE)
            tasks.append(lambda cs=cs: xbc_tile(cs))
            for rs in row_chunks:
                tasks.append(lambda rs=rs, cs=cs: conv(rs, cs))
        def keep_conv_tail():
            xbc_s[0:CONV_PAD, :] = xbc_s[SUB_BLOCK:SUB_BLOCK + CONV_PAD, :]

        tasks.append(keep_conv_tail)
        ssd_chunks = range(r0 // SSD_CHUNK, (r0 + SUB_BLOCK) // SSD_CHUNK)
        ssd_shared = {ci: {} for ci in ssd_chunks}
        for ci in ssd_chunks:
            tasks.append(lambda ci=ci: ssd_decays(ci, ssd_shared[ci]))
        tasks.append(q_tile)
        tasks.append(k_tile)
        for ci in ssd_chunks:
            for g in range(SSD_GROUPS):
                tasks.append(lambda ci=ci, g=g: ssd_weights(ci, g, ssd_shared[ci]))
        for ci in range(r0 // GLA_CHUNK, (r0 + SUB_BLOCK) // GLA_CHUNK):
            tasks.append(lambda ci=ci: gla_scores(ci))
        for col in range(0, GLA_VAL, PROJ_TILE):
            cs = slice(col, col + PROJ_TILE)
            tasks.append(lambda cs=cs: act_tile(gact_s, COL_G, cs))
        for col in range(0, SSD_INNER, PROJ_TILE):
            cs = slice(col, col + PROJ_TILE)
            tasks.append(lambda cs=cs: act_tile(zact_s, COL_Z, cs))
        for col in range(0, GLA_VAL, PROJ_TILE):
            cs = slice(col, col + PROJ_TILE)
            tasks.append(lambda cs=cs: v_tile(cs))
        return tasks

    def gla_scores(ci):
        rows = slice(ci * GLA_CHUNK, (ci + 1) * GLA_CHUNK)
        for hh in range(GLA_HEADS):
            ks = slice(hh * GLA_DK, (hh + 1) * GLA_DK)
            att = _dot_nt(qd_s[rows, ks], kd_s[rows, ks])
            att_s[hh, rows, :] = jnp.where(causal_g, att, 0.0).astype(BF16)

    def gla_chunk(ci):
        rows = slice(ci * GLA_CHUNK, (ci + 1) * GLA_CHUNK)
        decay = dec_s[ci:ci + 1, :]
        for hh in range(GLA_HEADS):
            ks = slice(hh * GLA_DK, (hh + 1) * GLA_DK)
            vs = slice(hh * GLA_DV, (hh + 1) * GLA_DV)
            v_h = v_s[rows, vs]
            o = _dot(att_s[hh, rows, :], v_h) + _dot_nt(qd_s[rows, ks], gstate_b_s[hh])
            state_t = gstate_s[hh] * decay[:, ks] + _dot_tn(v_h, ke_s[rows, ks])
            gstate_s[hh] = state_t
            gstate_b_s[hh] = state_t.astype(BF16)
            o = (o * lax.rsqrt(jnp.mean(o * o, axis=-1, keepdims=True) + NORM_EPS)) * gnw_ref[...]
            mix_s[rows, vs] = (o * gact_s[rows, vs].astype(F32)).astype(BF16)

    causal_s, tril_s = _tril(SSD_CHUNK, BF16)
    tril_s2 = jnp.concatenate([tril_s, tril_s], axis=1)
    a_neg = -jnp.exp(alog_ref[...])

    def expand_heads(v):
        hi, lo = _split(v)
        return _dot(jnp.concatenate([hi, lo], axis=1), expand_s[...])

    def ssd_decays(ci, shared):
        rows = slice(ci * SSD_CHUNK, (ci + 1) * SSD_CHUNK)
        dt_in = sm_s[rows, :] + dtb_ref[...]
        dt = jnp.maximum(dt_in, 0.0) + _log1p_exp_neg_abs(dt_in)
        dta_hi, dta_lo = _split(dt * a_neg)
        a_cum = _dot(tril_s2, jnp.concatenate([dta_hi, dta_lo], axis=0))
        a_last = a_cum[SSD_CHUNK - 1:SSD_CHUNK, :]
        decay_in_x = expand_heads(jnp.exp(a_cum))
        decin_s[rows, :] = decay_in_x
        cdec_s[ci:ci + 1, :] = decay_in_x[SSD_CHUNK - 1:SSD_CHUNK, :]
        xss_s[rows, :] = (u_s[rows, :] * expand_heads(jnp.exp(a_last - a_cum) * dt)).astype(BF16)
        shared.update(a_cum=a_cum, src_t=(a_cum - jnp.log(dt)).T)

    def ssd_weights(ci, g, shared):
        rows = slice(ci * SSD_CHUNK, (ci + 1) * SSD_CHUNK)
        b_g = bc_s[rows, g * SSD_STATE:(g + 1) * SSD_STATE]
        c_g = bc_s[rows, (SSD_GROUPS + g) * SSD_STATE:(SSD_GROUPS + g + 1) * SSD_STATE]
        a_cum, src_t = shared["a_cum"], shared["src_t"]
        cb = _dot_nt(c_g, b_g)
        for hl in range(SSD_HEADS_PER_GROUP):
            head = g * SSD_HEADS_PER_GROUP + hl
            lane = DT_LANE + head
            seg = jnp.where(causal_s, a_cum[:, lane:lane + 1] - src_t[lane:lane + 1, :], MASKED_LOG)
            w_s[rows, head * SSD_CHUNK:(head + 1) * SSD_CHUNK] = (jnp.exp(seg) * cb).astype(BF16)

    def ssd_group(ci, g):
        rows = slice(ci * SSD_CHUNK, (ci + 1) * SSD_CHUNK)
        gs = slice(g * SSD_GROUP_WIDTH, (g + 1) * SSD_GROUP_WIDTH)
        b_g = bc_s[rows, g * SSD_STATE:(g + 1) * SSD_STATE]
        c_g = bc_s[rows, (SSD_GROUPS + g) * SSD_STATE:(SSD_GROUPS + g + 1) * SSD_STATE]
        y_g = _dot(c_g, sstate_b_s[g]) * decin_s[rows, gs]
        state = sstate_s[g] * cdec_s[ci:ci + 1, gs] + _dot_tn(b_g, xss_s[rows, gs])
        sstate_s[g] = state
        sstate_b_s[g] = state.astype(BF16)
        y_tiles = []
        for pair in range(SSD_HEADS_PER_GROUP // 2):
            ts = slice(gs.start + pair * LANES, gs.start + (pair + 1) * LANES)
            head = g * SSD_HEADS_PER_GROUP + 2 * pair
            x_pair = jnp.concatenate([xlo_s[rows, ts], xhi_s[rows, ts]], axis=0)
            y_tiles.append(_dot(w_s[rows, head * SSD_CHUNK:(head + 2) * SSD_CHUNK], x_pair))
        y_g = y_g + jnp.concatenate(y_tiles, axis=1) + dskip_ref[:, gs] * u_s[rows, gs]
        y_g = y_g * zact_s[rows, gs].astype(F32)
        y_g = (y_g * lax.rsqrt(jnp.mean(y_g * y_g, axis=-1, keepdims=True) + NORM_EPS)) * snw_ref[:, gs]
        mix_s[rows, GLA_VAL + g * SSD_GROUP_WIDTH:GLA_VAL + (g + 1) * SSD_GROUP_WIDTH] = y_g.astype(BF16)

    def out_rows(rs):
        mixed = _dot(mix_s[rs, :], wout_ref[...])
        mixed = (mixed * lax.rsqrt(jnp.mean(mixed * mixed, axis=-1, keepdims=True) + NORM_EPS)) * postw_ref[...]
        out_ref[0, rs, :] = x_ref[0, rs, :] + gate * mixed

    def recur_tasks(r0):
        tasks = []
        gla_per_ssd = SSD_CHUNK // GLA_CHUNK
        for ci in range(r0 // SSD_CHUNK, (r0 + SUB_BLOCK) // SSD_CHUNK):
            for cc in range(gla_per_ssd):
                tasks.append(lambda ci=ci, cc=cc: gla_chunk(ci * gla_per_ssd + cc))
            for g in range(SSD_GROUPS):
                tasks.append(lambda ci=ci, g=g: ssd_group(ci, g))
        tasks.append(lambda: out_rows(slice(r0, r0 + SUB_BLOCK)))
        return tasks

    def issue_alternating(a, b):
        order = sorted([((i + 0.0) / len(a), 0, i) for i in range(len(a))]
                       + [((i + 1.0) / len(b), 1, i) for i in range(len(b))])
        for _, which, i in order:
            (a, b)[which][i]()

    sub_starts = list(range(0, tb, SUB_BLOCK))
    for task in project_tasks(sub_starts[0]):
        task()
    for prev, cur in zip(sub_starts[:-1], sub_starts[1:]):
        issue_alternating(project_tasks(cur), recur_tasks(prev))
    for task in recur_tasks(sub_starts[-1]):
        task()


def _const_spec(shape):
    zeros = (0,) * len(shape)
    return pl.BlockSpec(shape, lambda b, t: zeros, pipeline_mode=pl.Buffered(1))


def _layer(x, ada, vecs, wmain, wsm, w2, wout):
    bsz, seq, _ = x.shape
    tb = TIME_BLOCK
    consts = (ada, vecs, wmain, wsm, w2, wout)
    return pl.pallas_call(
        _layer_kernel,
        grid=(bsz, seq // tb),
        in_specs=[pl.BlockSpec((1, tb, D_MODEL), lambda b, t: (b, t, 0))]
        + [_const_spec(a.shape) for a in consts],
        out_specs=pl.BlockSpec((1, tb, D_MODEL), lambda b, t: (b, t, 0)),
        out_shape=jax.ShapeDtypeStruct(x.shape, x.dtype),
        scratch_shapes=[
            pltpu.VMEM((tb, D_MODEL), BF16),
            pltpu.VMEM((tb, LANES), F32),
            pltpu.VMEM((SUB_BLOCK + CONV_PAD, SSD_CONV_DIM), F32),
            pltpu.VMEM((tb, SSD_INNER), F32),
            pltpu.VMEM((tb, GLA_KEY), F32),
            pltpu.VMEM((tb, GLA_KEY), BF16),
            pltpu.VMEM((tb, GLA_KEY), BF16),
            pltpu.VMEM((tb, GLA_KEY), BF16),
            pltpu.VMEM((tb // GLA_CHUNK, GLA_KEY), F32),
            pltpu.VMEM((tb, GLA_VAL), BF16),
            pltpu.VMEM((tb, GLA_VAL), BF16),
            pltpu.VMEM((tb, SSD_INNER), BF16),
            pltpu.VMEM((tb, D_MIX), BF16),
            pltpu.VMEM((GLA_HEADS, GLA_DV, GLA_DK), F32),
            pltpu.VMEM((SSD_GROUPS, SSD_STATE, SSD_GROUP_WIDTH), F32),
            pltpu.VMEM((2 * LANES, GLA_KEY), BF16),
            pltpu.VMEM((LANES, GLA_KEY), BF16),
            pltpu.VMEM((2 * LANES, SSD_INNER), BF16),
            pltpu.VMEM((tb, SSD_INNER), BF16),
            pltpu.VMEM((tb, SSD_INNER), BF16),
            pltpu.VMEM((tb, 2 * SSD_GROUPS * SSD_STATE), BF16),
            pltpu.VMEM((GLA_HEADS, GLA_DV, GLA_DK), BF16),
            pltpu.VMEM((SSD_GROUPS, SSD_STATE, SSD_GROUP_WIDTH), BF16),
            pltpu.VMEM((GLA_HEADS, tb, GLA_CHUNK), BF16),
            pltpu.VMEM((tb, SSD_HEADS * SSD_CHUNK), BF16),
            pltpu.VMEM((tb, SSD_INNER), BF16),
            pltpu.VMEM((tb, SSD_INNER), F32),
            pltpu.VMEM((tb // SSD_CHUNK, SSD_INNER), F32),
        ],
        compiler_params=pltpu.CompilerParams(
            dimension_semantics=("arbitrary", "arbitrary"),
            vmem_limit_bytes=VMEM_LIMIT_BYTES),
        name="hybrid_layer",
    )(x, *consts)


def _prep_kernel(wt_ref, gate_ref, dt_ref, wmain_ref, wsm_ref):
    wmain_ref[...] = wt_ref[...].T.astype(BF16)

    @pl.when(pl.program_id(0) == 0)
    def _():
        narrow_t = jnp.concatenate(
            [gate_ref[...], dt_ref[...], jnp.zeros((LANES - DT_LANE - SSD_HEADS, D_MODEL), F32)], axis=0)
        hi, lo = _split(narrow_t.T)
        wsm_ref[:, 0:LANES] = hi
        wsm_ref[:, LANES:2 * LANES] = lo


def _prep_weights(w_t):
    def src_row(j):
        return pl.multiple_of(j * PROJ_TILE + jnp.where(j * PROJ_TILE >= IN_GATE, GLA_GATE_RANK, 0), SUBLANES)

    return pl.pallas_call(
        _prep_kernel,
        grid=(W_MAIN_COLS // PROJ_TILE,),
        in_specs=[pl.BlockSpec((pl.Element(PROJ_TILE), pl.Element(D_MODEL)), lambda j: (src_row(j), 0)),
                  pl.BlockSpec((pl.Element(GLA_GATE_RANK), pl.Element(D_MODEL)), lambda j: (IN_GATE, 0)),
                  pl.BlockSpec((pl.Element(SSD_HEADS), pl.Element(D_MODEL)), lambda j: (IN_DT, 0))],
        out_specs=[pl.BlockSpec((D_MODEL, PROJ_TILE), lambda j: (0, j)),
                   pl.BlockSpec((D_MODEL, 2 * LANES), lambda j: (0, 0))],
        out_shape=[jax.ShapeDtypeStruct((D_MODEL, W_MAIN_COLS), BF16),
                   jax.ShapeDtypeStruct((D_MODEL, 2 * LANES), BF16)],
        compiler_params=pltpu.CompilerParams(dimension_semantics=("arbitrary",)),
        name="prep_weights",
    )(w_t, w_t, w_t)


def _row(v, offset=0):
    tail = VEC_COLS - offset - v.shape[0]
    return [jnp.zeros((offset,), F32), v.astype(F32), jnp.zeros((tail,), F32)]


def kernel(x, c, ada_w, ada_b, pre_norm_w, w_in, gla_gate_w2, gla_gate_b, gla_norm_w, conv_w, conv_b,
           dt_bias, a_log, d_skip, ssd_norm_w, w_out, post_norm_w):
    bsz, seq, _ = x.shape
    assert seq % TIME_BLOCK == 0 and bsz <= SUBLANES
    depth = ada_w.shape[0]
    c_pad = jnp.pad(c, ((0, SUBLANES - bsz), (0, 0)))
    for i in range(depth):
        ada = _ada(c_pad, ada_w[i], ada_b[i][None, :])
        wmain, wsm2 = _prep_weights(w_in[i].T)
        rows = ([_row(pre_norm_w[i]), _row(gla_gate_b[i]), _row(gla_norm_w[i]), _row(conv_b[i])]
                + [_row(conv_w[i, kk]) for kk in range(SSD_CONV)]
                + [_row(dt_bias[i], DT_LANE), _row(a_log[i], DT_LANE),
                   _row(jnp.repeat(d_skip[i], SSD_HEAD_DIM)), _row(ssd_norm_w[i]), _row(post_norm_w[i])]
                + [[jnp.zeros(((VEC_ROWS - VEC_USED) * VEC_COLS,), F32)]])
        vecs = jnp.concatenate([piece for row in rows for piece in row]).reshape(VEC_ROWS, VEC_COLS)
        w2 = jnp.pad(gla_gate_w2[i], ((0, LANES - GLA_GATE_RANK), (0, 0)))
        x = _layer(x, ada, vecs, wmain, wsm2, w2, w_out[i].astype(BF16))
    return x
```

```python
import jax
import jax.numpy as jnp
from jax import lax
from jax.experimental import pallas as pl
from jax.experimental.pallas import tpu as pltpu

F32 = jnp.float32
BF16 = jnp.bfloat16

NORM_EPS = 1e-6
D_MODEL = 1024
GLA_HEADS = 4
GLA_DK = 128
GLA_DV = 256
GLA_KEY = GLA_HEADS * GLA_DK
GLA_VAL = GLA_HEADS * GLA_DV
GLA_GATE_RANK = 16
GLA_TAU = 16.0
GLA_CHUNK = 64
SSD_INNER = 1024
SSD_HEAD_DIM = 64
SSD_HEADS = 16
SSD_GROUPS = 2
SSD_HEADS_PER_GROUP = SSD_HEADS // SSD_GROUPS
SSD_GROUP_WIDTH = SSD_INNER // SSD_GROUPS
SSD_STATE = 128
SSD_CONV = 4
SSD_CHUNK = 128
SSD_CONV_DIM = SSD_INNER + 2 * SSD_GROUPS * SSD_STATE
D_MIX = GLA_VAL + SSD_INNER

LANES = 128
SUBLANES = 8

TIME_BLOCK = 1024
ROW_CHUNK = 128
PROJ_TILE = 512
SUB_BLOCK = 256
COL_QK = 0
COL_V = COL_QK + 2 * GLA_KEY
COL_G = COL_V + GLA_VAL
COL_Z = COL_G + GLA_VAL
COL_XBC = COL_Z + SSD_INNER
W_MAIN_COLS = COL_XBC + SSD_CONV_DIM
DT_LANE = GLA_GATE_RANK
assert DT_LANE + SSD_HEADS <= LANES
IN_GATE = 2 * GLA_KEY + 2 * GLA_VAL
IN_Z = IN_GATE + GLA_GATE_RANK
IN_DT = IN_Z + SSD_INNER + SSD_CONV_DIM
D_PROJ = IN_DT + SSD_HEADS
assert IN_GATE % PROJ_TILE == 0 and W_MAIN_COLS % PROJ_TILE == 0 and SSD_INNER % PROJ_TILE == 0
assert 2 * SSD_HEAD_DIM == LANES
MASKED_LOG = -1e30
(VEC_PRE_W, VEC_GATE_B, VEC_GLA_NORM, VEC_CONV_B, VEC_CONV_W, VEC_DT_BIAS, VEC_A_LOG,
 VEC_D_SKIP, VEC_SSD_NORM, VEC_POST_W) = (0, 1, 2, 3, 4, 4 + SSD_CONV, 5 + SSD_CONV, 6 + SSD_CONV,
                                          7 + SSD_CONV, 8 + SSD_CONV)
VEC_USED = VEC_POST_W + 1
VEC_ROWS = -(-VEC_USED // SUBLANES) * SUBLANES
VEC_COLS = SSD_CONV_DIM
CONV_PAD = SUBLANES
VMEM_LIMIT_BYTES = 62 * 1024 * 1024


def _dot(a, b):
    return jnp.dot(a, b, preferred_element_type=F32)


def _dot_nt(a, b):
    return lax.dot_general(a, b, (((1,), (1,)), ((), ())), preferred_element_type=F32)


def _dot_tn(a, b):
    return lax.dot_general(a, b, (((0,), (0,)), ((), ())), preferred_element_type=F32)


def _split(x):
    hi = x.astype(BF16)
    lo = (x - hi.astype(F32)).astype(BF16)
    return hi, lo


def _silu(x):
    hx = 0.5 * x
    return hx + hx * jnp.tanh(hx)


def _log1p_exp_neg_abs(x):
    return jnp.log(1.0 + jnp.exp(-jnp.abs(x)))


def _tril(n, dtype):
    r = lax.broadcasted_iota(jnp.int32, (n, n), 0)
    c = lax.broadcasted_iota(jnp.int32, (n, n), 1)
    return r >= c, (r >= c).astype(dtype)


def _ada_kernel(c_ref, w_ref, b_ref, o_ref):
    c = c_ref[...]
    ca_hi, ca_lo = _split(_silu(c))
    w_hi, w_lo = _split(w_ref[...])
    o_ref[...] = _dot(ca_hi, w_hi) + _dot(ca_hi, w_lo) + _dot(ca_lo, w_hi) + b_ref[...]


def _ada(c_pad, ada_w, ada_b):
    rows = c_pad.shape[0]
    return pl.pallas_call(
        _ada_kernel,
        grid=(3,),
        in_specs=[
            pl.BlockSpec((rows, D_MODEL), lambda j: (0, 0)),
            pl.BlockSpec((D_MODEL, D_MODEL), lambda j: (0, j)),
            pl.BlockSpec((1, D_MODEL), lambda j: (0, j)),
        ],
        out_specs=pl.BlockSpec((rows, D_MODEL), lambda j: (0, j)),
        out_shape=jax.ShapeDtypeStruct((rows, 3 * D_MODEL), F32),
        compiler_params=pltpu.CompilerParams(dimension_semantics=("arbitrary",)),
        name="adaln",
    )(c_pad, ada_w, ada_b)


def _layer_kernel(x_ref, ada_ref, vec_ref, wmain_ref, wsm_ref, w2_ref, wout_ref,
                  out_ref,
                  h_s, sm_s, xbc_s, u_s, b_s, qd_s, kd_s, ke_s, dec_s, v_s, gact_s, zact_s, mix_s,
                  gstate_s, sstate_s, w2a_s, w2b_s, expand_s, xlo_s, xhi_s, bc_s,
                  gstate_b_s, sstate_b_s, att_s, w_s, xss_s, decin_s, cdec_s):
    tb = TIME_BLOCK
    t = pl.program_id(1)

    @pl.when(t == 0)
    def _():
        gstate_s[...] = jnp.zeros_like(gstate_s)
        sstate_s[...] = jnp.zeros_like(sstate_s)
        gstate_b_s[...] = jnp.zeros_like(gstate_b_s)
        sstate_b_s[...] = jnp.zeros_like(sstate_b_s)
        xbc_s[0:CONV_PAD, :] = jnp.zeros((CONV_PAD, SSD_CONV_DIM), F32)
        w2_hi, w2_lo = _split(w2_ref[...])
        w2a_s[0:LANES, :] = w2_hi
        w2a_s[LANES:2 * LANES, :] = w2_hi
        w2b_s[...] = w2_lo
        head_row = lax.broadcasted_iota(jnp.int32, (2 * LANES, SSD_INNER), 0) % LANES
        head_of_col = lax.broadcasted_iota(jnp.int32, (2 * LANES, SSD_INNER), 1) // SSD_HEAD_DIM + DT_LANE
        expand_s[...] = (head_row == head_of_col).astype(BF16)

    def vec_view(row, width, rows=1):
        return vec_ref.at[row:row + rows, 0:width]

    prew_ref = vec_view(VEC_PRE_W, D_MODEL)
    gb_ref = vec_view(VEC_GATE_B, GLA_KEY)
    gnw_ref = vec_view(VEC_GLA_NORM, GLA_DV)
    cb_ref = vec_view(VEC_CONV_B, SSD_CONV_DIM)
    cw_ref = vec_view(VEC_CONV_W, SSD_CONV_DIM, SSD_CONV)
    dtb_ref = vec_view(VEC_DT_BIAS, LANES)
    alog_ref = vec_view(VEC_A_LOG, LANES)
    dskip_ref = vec_view(VEC_D_SKIP, SSD_INNER)
    snw_ref = vec_view(VEC_SSD_NORM, SSD_INNER)
    postw_ref = vec_view(VEC_POST_W, D_MODEL)

    ada = ada_ref[pl.ds(pl.program_id(0), 1), :]
    shift = ada[:, 0:D_MODEL]
    scale1 = 1.0 + ada[:, D_MODEL:2 * D_MODEL]
    gate = ada[:, 2 * D_MODEL:3 * D_MODEL]
    causal_g, tril_g = _tril(GLA_CHUNK, BF16)
    tril_g2 = jnp.concatenate([tril_g, tril_g], axis=1)

    def project_tasks(r0):
        sub = slice(r0, r0 + SUB_BLOCK)
        row_chunks = [slice(r, r + ROW_CHUNK) for r in range(r0, r0 + SUB_BLOCK, ROW_CHUNK)]
        tasks = []

        def proj(col, width):
            return _dot(h_s[sub, :], wmain_ref[:, col:col + width])

        def prenorm(rs):
            x = x_ref[0, rs, :]
            ms = jnp.mean(x * x, axis=-1, keepdims=True)
            h = (x * lax.rsqrt(ms + NORM_EPS)) * prew_ref[...] * scale1 + shift
            h_hi = h.astype(BF16)
            h_lo = (h - h_hi.astype(F32)).astype(BF16)
            h_s[rs, :] = h_hi
            both = _dot(h_hi, wsm_ref[...])
            sm_s[rs, :] = both[:, 0:LANES] + both[:, LANES:2 * LANES] + _dot(h_lo, wsm_ref[:, 0:LANES])

        def forget_gate(rs):
            sm_hi, sm_lo = _split(sm_s[rs, :])
            sm_both = jnp.concatenate([sm_hi, sm_lo], axis=1)
            gl = _dot(sm_both, w2a_s[...]) + _dot(sm_hi, w2b_s[...]) + gb_ref[...]
            la = (jnp.minimum(gl, 0.0) - _log1p_exp_neg_abs(gl)) * (1.0 / GLA_TAU)
            for cc in range(ROW_CHUNK // GLA_CHUNK):
                la_hi, la_lo = _split(la[cc * GLA_CHUNK:(cc + 1) * GLA_CHUNK, :])
                b = _dot(tril_g2, jnp.concatenate([la_hi, la_lo], axis=0))
                c0 = rs.start + cc * GLA_CHUNK
                b_s[c0:c0 + GLA_CHUNK, :] = b
                ci = c0 // GLA_CHUNK
                dec_s[ci:ci + 1, :] = jnp.exp(b[GLA_CHUNK - 1:GLA_CHUNK, :])

        def xbc_tile(cs):
            xbc_s[CONV_PAD:CONV_PAD + SUB_BLOCK, cs] = proj(COL_XBC + cs.start, PROJ_TILE)

        def conv(rs, cs):
            a = xbc_s[rs.start - r0:rs.start - r0 + ROW_CHUNK + CONV_PAD, cs]
            acc = cw_ref[0:1, cs] * a
            for kk in range(1, SSD_CONV):
                acc = pltpu.roll(acc, 1, 0) + cw_ref[kk:kk + 1, cs] * a
            u = _silu(acc[CONV_PAD:, :] + cb_ref[:, cs])
            if cs.start < SSD_INNER:
                u_s[rs, cs] = u
                first_head = lax.broadcasted_iota(jnp.int32, u.shape, 1) % LANES < SSD_HEAD_DIM
                xlo_s[rs, cs] = jnp.where(first_head, u, 0.0).astype(BF16)
                xhi_s[rs, cs] = jnp.where(first_head, 0.0, u).astype(BF16)
            else:
                bc_s[rs, cs.start - SSD_INNER:cs.stop - SSD_INNER] = u.astype(BF16)

        def q_tile():
            q = proj(COL_QK, GLA_KEY)
            qd_s[sub, :] = (q * (GLA_DK ** -0.5) * jnp.exp(b_s[sub, :])).astype(BF16)

        def k_tile():
            k = proj(COL_QK + GLA_KEY, GLA_KEY)
            for cc in range(SUB_BLOCK // GLA_CHUNK):
                rows = slice(r0 + cc * GLA_CHUNK, r0 + (cc + 1) * GLA_CHUNK)
                b = b_s[rows, :]
                k_c = k[cc * GLA_CHUNK:(cc + 1) * GLA_CHUNK, :]
                kd_s[rows, :] = (k_c * jnp.exp(-b)).astype(BF16)
                ke_s[rows, :] = (k_c * jnp.exp(b[GLA_CHUNK - 1:GLA_CHUNK, :] - b)).astype(BF16)

        def act_tile(dst_s, col, cs):
            dst_s[sub, cs] = _silu(proj(col + cs.start, PROJ_TILE).astype(BF16))

        def v_tile(cs):
            v_s[sub, cs] = proj(COL_V + cs.start, PROJ_TILE).astype(BF16)

        for rs in row_chunks:
            tasks.append(lambda rs=rs: prenorm(rs))
        for rs in row_chunks:
            tasks.append(lambda rs=rs: forget_gate(rs))
        for col in range(0, SSD_CONV_DIM, PROJ_TILE):
            cs = slice(col, col + PROJ_TILE)
            tasks.append(lambda cs=cs: xbc_tile(cs))
            for rs in row_chunks:
                tasks.append(lambda rs=rs, cs=cs: conv(rs, cs))
        def keep_conv_tail():
            xbc_s[0:CONV_PAD, :] = xbc_s[SUB_BLOCK:SUB_BLOCK + CONV_PAD, :]

        tasks.append(keep_conv_tail)
        ssd_chunks = range(r0 // SSD_CHUNK, (r0 + SUB_BLOCK) // SSD_CHUNK)
        ssd_shared = {ci: {} for ci in ssd_chunks}
        for ci in ssd_chunks:
            tasks.append(lambda ci=ci: ssd_decays(ci, ssd_shared[ci]))
        tasks.append(q_tile)
        tasks.append(k_tile)
        for ci in ssd_chunks:
            for g in range(SSD_GROUPS):
                tasks.append(lambda ci=ci, g=g: ssd_weights(ci, g, ssd_shared[ci]))
        for ci in range(r0 // GLA_CHUNK, (r0 + SUB_BLOCK) // GLA_CHUNK):
            tasks.append(lambda ci=ci: gla_scores(ci))
        for col in range(0, GLA_VAL, PROJ_TILE):
            cs = slice(col, col + PROJ_TILE)
            tasks.append(lambda cs=cs: act_tile(gact_s, COL_G, cs))
        for col in range(0, SSD_INNER, PROJ_TILE):
            cs = slice(col, col + PROJ_TILE)
            tasks.append(lambda cs=cs: act_tile(zact_s, COL_Z, cs))
        for col in range(0, GLA_VAL, PROJ_TILE):
            cs = slice(col, col + PROJ_TILE)
            tasks.append(lambda cs=cs: v_tile(cs))
        return tasks

    def gla_scores(ci):
        rows = slice(ci * GLA_CHUNK, (ci + 1) * GLA_CHUNK)
        for hh in range(GLA_HEADS):
            ks = slice(hh * GLA_DK, (hh + 1) * GLA_DK)
            att = _dot_nt(qd_s[rows, ks], kd_s[rows, ks])
            att_s[hh, rows, :] = jnp.where(causal_g, att, 0.0).astype(BF16)

    def gla_chunk(ci):
        rows = slice(ci * GLA_CHUNK, (ci + 1) * GLA_CHUNK)
        decay = dec_s[ci:ci + 1, :]
        for hh in range(GLA_HEADS):
            ks = slice(hh * GLA_DK, (hh + 1) * GLA_DK)
            vs = slice(hh * GLA_DV, (hh + 1) * GLA_DV)
            v_h = v_s[rows, vs]
            o = _dot(att_s[hh, rows, :], v_h) + _dot_nt(qd_s[rows, ks], gstate_b_s[hh])
            state_t = gstate_s[hh] * decay[:, ks] + _dot_tn(v_h, ke_s[rows, ks])
            gstate_s[hh] = state_t
            gstate_b_s[hh] = state_t.astype(BF16)
            o = (o * lax.rsqrt(jnp.mean(o * o, axis=-1, keepdims=True) + NORM_EPS)) * gnw_ref[...]
            mix_s[rows, vs] = (o * gact_s[rows, vs].astype(F32)).astype(BF16)

    causal_s, tril_s = _tril(SSD_CHUNK, BF16)
    tril_s2 = jnp.concatenate([tril_s, tril_s], axis=1)
    a_neg = -jnp.exp(alog_ref[...])

    def expand_heads(v):
        hi, lo = _split(v)
        return _dot(jnp.concatenate([hi, lo], axis=1), expand_s[...])

    def ssd_decays(ci, shared):
        rows = slice(ci * SSD_CHUNK, (ci + 1) * SSD_CHUNK)
        dt_in = sm_s[rows, :] + dtb_ref[...]
        dt = jnp.maximum(dt_in, 0.0) + _log1p_exp_neg_abs(dt_in)
        dta_hi, dta_lo = _split(dt * a_neg)
        a_cum = _dot(tril_s2, jnp.concatenate([dta_hi, dta_lo], axis=0))
        a_last = a_cum[SSD_CHUNK - 1:SSD_CHUNK, :]
        decay_in_x = expand_heads(jnp.exp(a_cum))
        decin_s[rows, :] = decay_in_x
        cdec_s[ci:ci + 1, :] = decay_in_x[SSD_CHUNK - 1:SSD_CHUNK, :]
        xss_s[rows, :] = (u_s[rows, :] * expand_heads(jnp.exp(a_last - a_cum) * dt)).astype(BF16)
        shared.update(a_cum=a_cum, src_t=(a_cum - jnp.log(dt)).T)

    def ssd_weights(ci, g, shared):
        rows = slice(ci * SSD_CHUNK, (ci + 1) * SSD_CHUNK)
        b_g = bc_s[rows, g * SSD_STATE:(g + 1) * SSD_STATE]
        c_g = bc_s[rows, (SSD_GROUPS + g) * SSD_STATE:(SSD_GROUPS + g + 1) * SSD_STATE]
        a_cum, src_t = shared["a_cum"], shared["src_t"]
        cb = _dot_nt(c_g, b_g)
        for hl in range(SSD_HEADS_PER_GROUP):
            head = g * SSD_HEADS_PER_GROUP + hl
            lane = DT_LANE + head
            seg = jnp.where(causal_s, a_cum[:, lane:lane + 1] - src_t[lane:lane + 1, :], MASKED_LOG)
            w_s[rows, head * SSD_CHUNK:(head + 1) * SSD_CHUNK] = (jnp.exp(seg) * cb).astype(BF16)

    def ssd_group(ci, g):
        rows = slice(ci * SSD_CHUNK, (ci + 1) * SSD_CHUNK)
        gs = slice(g * SSD_GROUP_WIDTH, (g + 1) * SSD_GROUP_WIDTH)
        b_g = bc_s[rows, g * SSD_STATE:(g + 1) * SSD_STATE]
        c_g = bc_s[rows, (SSD_GROUPS + g) * SSD_STATE:(SSD_GROUPS + g + 1) * SSD_STATE]
        y_g = _dot(c_g, sstate_b_s[g]) * decin_s[rows, gs]
        state = sstate_s[g] * cdec_s[ci:ci + 1, gs] + _dot_tn(b_g, xss_s[rows, gs])
        sstate_s[g] = state
        sstate_b_s[g] = state.astype(BF16)
        y_tiles = []
        for pair in range(SSD_HEADS_PER_GROUP // 2):
            ts = slice(gs.start + pair * LANES, gs.start + (pair + 1) * LANES)
            head = g * SSD_HEADS_PER_GROUP + 2 * pair
            x_pair = jnp.concatenate([xlo_s[rows, ts], xhi_s[rows, ts]], axis=0)
            y_tiles.append(_dot(w_s[rows, head * SSD_CHUNK:(head + 2) * SSD_CHUNK], x_pair))
        y_g = y_g + jnp.concatenate(y_tiles, axis=1) + dskip_ref[:, gs] * u_s[rows, gs]
        y_g = y_g * zact_s[rows, gs].astype(F32)
        y_g = (y_g * lax.rsqrt(jnp.mean(y_g * y_g, axis=-1, keepdims=True) + NORM_EPS)) * snw_ref[:, gs]
        mix_s[rows, GLA_VAL + g * SSD_GROUP_WIDTH:GLA_VAL + (g + 1) * SSD_GROUP_WIDTH] = y_g.astype(BF16)

    def out_rows(rs):
        mixed = _dot(mix_s[rs, :], wout_ref[...])
        mixed = (mixed * lax.rsqrt(jnp.mean(mixed * mixed, axis=-1, keepdims=True) + NORM_EPS)) * postw_ref[...]
        out_ref[0, rs, :] = x_ref[0, rs, :] + gate * mixed

    def recur_tasks(r0):
        tasks = []
        gla_per_ssd = SSD_CHUNK // GLA_CHUNK
        for ci in range(r0 // SSD_CHUNK, (r0 + SUB_BLOCK) // SSD_CHUNK):
            for cc in range(gla_per_ssd):
                tasks.append(lambda ci=ci, cc=cc: gla_chunk(ci * gla_per_ssd + cc))
            for g in range(SSD_GROUPS):
                tasks.append(lambda ci=ci, g=g: ssd_group(ci, g))
        tasks.append(lambda: out_rows(slice(r0, r0 + SUB_BLOCK)))
        return tasks

    def issue_alternating(a, b):
        order = sorted([((i + 0.75) / len(a), 0, i) for i in range(len(a))]
                       + [((i + 0.25) / len(b), 1, i) for i in range(len(b))])
        for _, which, i in order:
            (a, b)[which][i]()

    sub_starts = list(range(0, tb, SUB_BLOCK))
    for task in project_tasks(sub_starts[0]):
        task()
    for prev, cur in zip(sub_starts[:-1], sub_starts[1:]):
        issue_alternating(project_tasks(cur), recur_tasks(prev))
    for task in recur_tasks(sub_starts[-1]):
        task()


def _const_spec(shape):
    zeros = (0,) * len(shape)
    return pl.BlockSpec(shape, lambda b, t: zeros, pipeline_mode=pl.Buffered(1))


def _layer(x, ada, vecs, wmain, wsm, w2, wout):
    bsz, seq, _ = x.shape
    tb = TIME_BLOCK
    consts = (ada, vecs, wmain, wsm, w2, wout)
    return pl.pallas_call(
        _layer_kernel,
        grid=(bsz, seq // tb),
        in_specs=[pl.BlockSpec((1, tb, D_MODEL), lambda b, t: (b, t, 0))]
        + [_const_spec(a.shape) for a in consts],
        out_specs=pl.BlockSpec((1, tb, D_MODEL), lambda b, t: (b, t, 0)),
        out_shape=jax.ShapeDtypeStruct(x.shape, x.dtype),
        scratch_shapes=[
            pltpu.VMEM((tb, D_MODEL), BF16),
            pltpu.VMEM((tb, LANES), F32),
            pltpu.VMEM((SUB_BLOCK + CONV_PAD, SSD_CONV_DIM), F32),
            pltpu.VMEM((tb, SSD_INNER), F32),
            pltpu.VMEM((tb, GLA_KEY), F32),
            pltpu.VMEM((tb, GLA_KEY), BF16),
            pltpu.VMEM((tb, GLA_KEY), BF16),
            pltpu.VMEM((tb, GLA_KEY), BF16),
            pltpu.VMEM((tb // GLA_CHUNK, GLA_KEY), F32),
            pltpu.VMEM((tb, GLA_VAL), BF16),
            pltpu.VMEM((tb, GLA_VAL), BF16),
            pltpu.VMEM((tb, SSD_INNER), BF16),
            pltpu.VMEM((tb, D_MIX), BF16),
            pltpu.VMEM((GLA_HEADS, GLA_DV, GLA_DK), F32),
            pltpu.VMEM((SSD_GROUPS, SSD_STATE, SSD_GROUP_WIDTH), F32),
            pltpu.VMEM((2 * LANES, GLA_KEY), BF16),
            pltpu.VMEM((LANES, GLA_KEY), BF16),
            pltpu.VMEM((2 * LANES, SSD_INNER), BF16),
            pltpu.VMEM((tb, SSD_INNER), BF16),
            pltpu.VMEM((tb, SSD_INNER), BF16),
            pltpu.VMEM((tb, 2 * SSD_GROUPS * SSD_STATE), BF16),
            pltpu.VMEM((GLA_HEADS, GLA_DV, GLA_DK), BF16),
            pltpu.VMEM((SSD_GROUPS, SSD_STATE, SSD_GROUP_WIDTH), BF16),
            pltpu.VMEM((GLA_HEADS, tb, GLA_CHUNK), BF16),
            pltpu.VMEM((tb, SSD_HEADS * SSD_CHUNK), BF16),
            pltpu.VMEM((tb, SSD_INNER), BF16),
            pltpu.VMEM((tb, SSD_INNER), F32),
            pltpu.VMEM((tb // SSD_CHUNK, SSD_INNER), F32),
        ],
        compiler_params=pltpu.CompilerParams(
            dimension_semantics=("arbitrary", "arbitrary"),
            vmem_limit_bytes=VMEM_LIMIT_BYTES),
        name="hybrid_layer",
    )(x, *consts)


def _prep_kernel(wt_ref, gate_ref, dt_ref, wmain_ref, wsm_ref):
    wmain_ref[...] = wt_ref[...].T.astype(BF16)

    @pl.when(pl.program_id(0) == 0)
    def _():
        narrow_t = jnp.concatenate(
            [gate_ref[...], dt_ref[...], jnp.zeros((LANES - DT_LANE - SSD_HEADS, D_MODEL), F32)], axis=0)
        hi, lo = _split(narrow_t.T)
        wsm_ref[:, 0:LANES] = hi
        wsm_ref[:, LANES:2 * LANES] = lo


def _prep_weights(w_t):
    def src_row(j):
        return pl.multiple_of(j * PROJ_TILE + jnp.where(j * PROJ_TILE >= IN_GATE, GLA_GATE_RANK, 0), SUBLANES)

    return pl.pallas_call(
        _prep_kernel,
        grid=(W_MAIN_COLS // PROJ_TILE,),
        in_specs=[pl.BlockSpec((pl.Element(PROJ_TILE), pl.Element(D_MODEL)), lambda j: (src_row(j), 0)),
                  pl.BlockSpec((pl.Element(GLA_GATE_RANK), pl.Element(D_MODEL)), lambda j: (IN_GATE, 0)),
                  pl.BlockSpec((pl.Element(SSD_HEADS), pl.Element(D_MODEL)), lambda j: (IN_DT, 0))],
        out_specs=[pl.BlockSpec((D_MODEL, PROJ_TILE), lambda j: (0, j)),
                   pl.BlockSpec((D_MODEL, 2 * LANES), lambda j: (0, 0))],
        out_shape=[jax.ShapeDtypeStruct((D_MODEL, W_MAIN_COLS), BF16),
                   jax.ShapeDtypeStruct((D_MODEL, 2 * LANES), BF16)],
        compiler_params=pltpu.CompilerParams(dimension_semantics=("arbitrary",)),
        name="prep_weights",
    )(w_t, w_t, w_t)


def _row(v, offset=0):
    tail = VEC_COLS - offset - v.shape[0]
    return [jnp.zeros((offset,), F32), v.astype(F32), jnp.zeros((tail,), F32)]


def kernel(x, c, ada_w, ada_b, pre_norm_w, w_in, gla_gate_w2, gla_gate_b, gla_norm_w, conv_w, conv_b,
           dt_bias, a_log, d_skip, ssd_norm_w, w_out, post_norm_w):
    bsz, seq, _ = x.shape
    assert seq % TIME_BLOCK == 0 and bsz <= SUBLANES
    depth = ada_w.shape[0]
    c_pad = jnp.pad(c, ((0, SUBLANES - bsz), (0, 0)))
    for i in range(depth):
        ada = _ada(c_pad, ada_w[i], ada_b[i][None, :])
        wmain, wsm2 = _prep_weights(w_in[i].T)
        rows = ([_row(pre_norm_w[i]), _row(gla_gate_b[i]), _row(gla_norm_w[i]), _row(conv_b[i])]
                + [_row(conv_w[i, kk]) for kk in range(SSD_CONV)]
                + [_row(dt_bias[i], DT_LANE), _row(a_log[i], DT_LANE),
                   _row(jnp.repeat(d_skip[i], SSD_HEAD_DIM)), _row(ssd_norm_w[i]), _row(post_norm_w[i])]
                + [[jnp.zeros(((VEC_ROWS - VEC_USED) * VEC_COLS,), F32)]])
        vecs = jnp.concatenate([piece for row in rows for piece in row]).reshape(VEC_ROWS, VEC_COLS)
        w2 = jnp.pad(gla_gate_w2[i], ((0, LANES - GLA_GATE_RANK), (0, 0)))
        x = _layer(x, ada, vecs, wmain, wsm2, w2, w_out[i].astype(BF16))
    return x
```

```python
import jax
import jax.numpy as jnp
from jax import lax
from jax.experimental import pallas as pl
from jax.experimental.pallas import tpu as pltpu

F32 = jnp.float32
BF16 = jnp.bfloat16

NORM_EPS = 1e-6
D_MODEL = 1024
GLA_HEADS = 4
GLA_DK = 128
GLA_DV = 256
GLA_KEY = GLA_HEADS * GLA_DK
GLA_VAL = GLA_HEADS * GLA_DV
GLA_GATE_RANK = 16
GLA_TAU = 16.0
GLA_CHUNK = 64
SSD_INNER = 1024
SSD_HEAD_DIM = 64
SSD_HEADS = 16
SSD_GROUPS = 2
SSD_HEADS_PER_GROUP = SSD_HEADS // SSD_GROUPS
SSD_GROUP_WIDTH = SSD_INNER // SSD_GROUPS
SSD_STATE = 128
SSD_CONV = 4
SSD_CHUNK = 128
SSD_CONV_DIM = SSD_INNER + 2 * SSD_GROUPS * SSD_STATE
D_MIX = GLA_VAL + SSD_INNER

LANES = 128
SUBLANES = 8

TIME_BLOCK = 1024
ROW_CHUNK = 128
PROJ_TILE = 512
SUB_BLOCK = 256
COL_QK = 0
COL_V = COL_QK + 2 * GLA_KEY
COL_G = COL_V + GLA_VAL
COL_Z = COL_G + GLA_VAL
COL_XBC = COL_Z + SSD_INNER
W_MAIN_COLS = COL_XBC + SSD_CONV_DIM
DT_LANE = GLA_GATE_RANK
assert DT_LANE + SSD_HEADS <= LANES
IN_GATE = 2 * GLA_KEY + 2 * GLA_VAL
IN_Z = IN_GATE + GLA_GATE_RANK
IN_DT = IN_Z + SSD_INNER + SSD_CONV_DIM
D_PROJ = IN_DT + SSD_HEADS
assert IN_GATE % PROJ_TILE == 0 and W_MAIN_COLS % PROJ_TILE == 0 and SSD_INNER % PROJ_TILE == 0
assert 2 * SSD_HEAD_DIM == LANES
MASKED_LOG = -1e30
(VEC_PRE_W, VEC_GATE_B, VEC_GLA_NORM, VEC_CONV_B, VEC_CONV_W, VEC_DT_BIAS, VEC_A_LOG,
 VEC_D_SKIP, VEC_SSD_NORM, VEC_POST_W) = (0, 1, 2, 3, 4, 4 + SSD_CONV, 5 + SSD_CONV, 6 + SSD_CONV,
                                          7 + SSD_CONV, 8 + SSD_CONV)
VEC_USED = VEC_POST_W + 1
VEC_ROWS = -(-VEC_USED // SUBLANES) * SUBLANES
VEC_COLS = SSD_CONV_DIM
CONV_PAD = SUBLANES
VMEM_LIMIT_BYTES = 62 * 1024 * 1024


def _dot(a, b):
    return jnp.dot(a, b, preferred_element_type=F32)


def _dot_nt(a, b):
    return lax.dot_general(a, b, (((1,), (1,)), ((), ())), preferred_element_type=F32)


def _dot_tn(a, b):
    return lax.dot_general(a, b, (((0,), (0,)), ((), ())), preferred_element_type=F32)


def _split(x):
    hi = x.astype(BF16)
    lo = (x - hi.astype(F32)).astype(BF16)
    return hi, lo


def _silu(x):
    hx = 0.5 * x
    return hx + hx * jnp.tanh(hx)


def _log1p_exp_neg_abs(x):
    return jnp.log(1.0 + jnp.exp(-jnp.abs(x)))


def _tril(n, dtype):
    r = lax.broadcasted_iota(jnp.int32, (n, n), 0)
    c = lax.broadcasted_iota(jnp.int32, (n, n), 1)
    return r >= c, (r >= c).astype(dtype)


def _ada_kernel(c_ref, w_ref, b_ref, o_ref):
    c = c_ref[...]
    ca_hi, ca_lo = _split(_silu(c))
    w_hi, w_lo = _split(w_ref[...])
    o_ref[...] = _dot(ca_hi, w_hi) + _dot(ca_hi, w_lo) + _dot(ca_lo, w_hi) + b_ref[...]


def _ada(c_pad, ada_w, ada_b):
    rows = c_pad.shape[0]
    return pl.pallas_call(
        _ada_kernel,
        grid=(3,),
        in_specs=[
            pl.BlockSpec((rows, D_MODEL), lambda j: (0, 0)),
            pl.BlockSpec((D_MODEL, D_MODEL), lambda j: (0, j)),
            pl.BlockSpec((1, D_MODEL), lambda j: (0, j)),
        ],
        out_specs=pl.BlockSpec((rows, D_MODEL), lambda j: (0, j)),
        out_shape=jax.ShapeDtypeStruct((rows, 3 * D_MODEL), F32),
        compiler_params=pltpu.CompilerParams(dimension_semantics=("arbitrary",)),
        name="adaln",
    )(c_pad, ada_w, ada_b)


def _layer_kernel(x_ref, ada_ref, vec_ref, wmain_ref, wsm_ref, w2_ref, wout_ref,
                  out_ref,
                  h_s, sm_s, xbc_s, u_s, b_s, qd_s, kd_s, ke_s, dec_s, v_s, gact_s, zact_s, mix_s,
                  gstate_s, sstate_s, w2a_s, w2b_s, expand_s, xlo_s, xhi_s, bc_s,
                  gstate_b_s, sstate_b_s, att_s, w_s, xss_s, decin_s, cdec_s):
    tb = TIME_BLOCK
    t = pl.program_id(1)

    @pl.when(t == 0)
    def _():
        gstate_s[...] = jnp.zeros_like(gstate_s)
        sstate_s[...] = jnp.zeros_like(sstate_s)
        gstate_b_s[...] = jnp.zeros_like(gstate_b_s)
        sstate_b_s[...] = jnp.zeros_like(sstate_b_s)
        xbc_s[0:CONV_PAD, :] = jnp.zeros((CONV_PAD, SSD_CONV_DIM), F32)
        w2_hi, w2_lo = _split(w2_ref[...])
        w2a_s[0:LANES, :] = w2_hi
        w2a_s[LANES:2 * LANES, :] = w2_hi
        w2b_s[...] = w2_lo
        head_row = lax.broadcasted_iota(jnp.int32, (2 * LANES, SSD_INNER), 0) % LANES
        head_of_col = lax.broadcasted_iota(jnp.int32, (2 * LANES, SSD_INNER), 1) // SSD_HEAD_DIM + DT_LANE
        expand_s[...] = (head_row == head_of_col).astype(BF16)

    def vec_view(row, width, rows=1):
        return vec_ref.at[row:row + rows, 0:width]

    prew_ref = vec_view(VEC_PRE_W, D_MODEL)
    gb_ref = vec_view(VEC_GATE_B, GLA_KEY)
    gnw_ref = vec_view(VEC_GLA_NORM, GLA_DV)
    cb_ref = vec_view(VEC_CONV_B, SSD_CONV_DIM)
    cw_ref = vec_view(VEC_CONV_W, SSD_CONV_DIM, SSD_CONV)
    dtb_ref = vec_view(VEC_DT_BIAS, LANES)
    alog_ref = vec_view(VEC_A_LOG, LANES)
    dskip_ref = vec_view(VEC_D_SKIP, SSD_INNER)
    snw_ref = vec_view(VEC_SSD_NORM, SSD_INNER)
    postw_ref = vec_view(VEC_POST_W, D_MODEL)

    ada = ada_ref[pl.ds(pl.program_id(0), 1), :]
    shift = ada[:, 0:D_MODEL]
    scale1 = 1.0 + ada[:, D_MODEL:2 * D_MODEL]
    gate = ada[:, 2 * D_MODEL:3 * D_MODEL]
    causal_g, tril_g = _tril(GLA_CHUNK, BF16)
    tril_g2 = jnp.concatenate([tril_g, tril_g], axis=1)

    def project_tasks(r0):
        sub = slice(r0, r0 + SUB_BLOCK)
        row_chunks = [slice(r, r + ROW_CHUNK) for r in range(r0, r0 + SUB_BLOCK, ROW_CHUNK)]
        tasks = []

        def proj(col, width):
            return _dot(h_s[sub, :], wmain_ref[:, col:col + width])

        def prenorm(rs):
            x = x_ref[0, rs, :]
            ms = jnp.mean(x * x, axis=-1, keepdims=True)
            h = (x * lax.rsqrt(ms + NORM_EPS)) * prew_ref[...] * scale1 + shift
            h_hi = h.astype(BF16)
            h_lo = (h - h_hi.astype(F32)).astype(BF16)
            h_s[rs, :] = h_hi
            both = _dot(h_hi, wsm_ref[...])
            sm_s[rs, :] = both[:, 0:LANES] + both[:, LANES:2 * LANES] + _dot(h_lo, wsm_ref[:, 0:LANES])

        def forget_gate(rs):
            sm_hi, sm_lo = _split(sm_s[rs, :])
            sm_both = jnp.concatenate([sm_hi, sm_lo], axis=1)
            gl = _dot(sm_both, w2a_s[...]) + _dot(sm_hi, w2b_s[...]) + gb_ref[...]
            la = (jnp.minimum(gl, 0.0) - _log1p_exp_neg_abs(gl)) * (1.0 / GLA_TAU)
            for cc in range(ROW_CHUNK // GLA_CHUNK):
                la_hi, la_lo = _split(la[cc * GLA_CHUNK:(cc + 1) * GLA_CHUNK, :])
                b = _dot(tril_g2, jnp.concatenate([la_hi, la_lo], axis=0))
                c0 = rs.start + cc * GLA_CHUNK
                b_s[c0:c0 + GLA_CHUNK, :] = b
                ci = c0 // GLA_CHUNK
                dec_s[ci:ci + 1, :] = jnp.exp(b[GLA_CHUNK - 1:GLA_CHUNK, :])

        def xbc_tile(cs):
            xbc_s[CONV_PAD:CONV_PAD + SUB_BLOCK, cs] = proj(COL_XBC + cs.start, PROJ_TILE)

        def conv(rs, cs):
            a = xbc_s[rs.start - r0:rs.start - r0 + ROW_CHUNK + CONV_PAD, cs]
            acc = cw_ref[0:1, cs] * a
            for kk in range(1, SSD_CONV):
                acc = pltpu.roll(acc, 1, 0) + cw_ref[kk:kk + 1, cs] * a
            u = _silu(acc[CONV_PAD:, :] + cb_ref[:, cs])
            if cs.start < SSD_INNER:
                u_s[rs, cs] = u
                first_head = lax.broadcasted_iota(jnp.int32, u.shape, 1) % LANES < SSD_HEAD_DIM
                xlo_s[rs, cs] = jnp.where(first_head, u, 0.0).astype(BF16)
                xhi_s[rs, cs] = jnp.where(first_head, 0.0, u).astype(BF16)
            else:
                bc_s[rs, cs.start - SSD_INNER:cs.stop - SSD_INNER] = u.astype(BF16)

        def q_tile():
            q = proj(COL_QK, GLA_KEY)
            qd_s[sub, :] = (q * (GLA_DK ** -0.5) * jnp.exp(b_s[sub, :])).astype(BF16)

        def k_tile():
            k = proj(COL_QK + GLA_KEY, GLA_KEY)
            for cc in range(SUB_BLOCK // GLA_CHUNK):
                rows = slice(r0 + cc * GLA_CHUNK, r0 + (cc + 1) * GLA_CHUNK)
                b = b_s[rows, :]
                k_c = k[cc * GLA_CHUNK:(cc + 1) * GLA_CHUNK, :]
                kd_s[rows, :] = (k_c * jnp.exp(-b)).astype(BF16)
                ke_s[rows, :] = (k_c * jnp.exp(b[GLA_CHUNK - 1:GLA_CHUNK, :] - b)).astype(BF16)

        def act_tile(dst_s, col, cs):
            dst_s[sub, cs] = _silu(proj(col + cs.start, PROJ_TILE).astype(BF16))

        def v_tile(cs):
            v_s[sub, cs] = proj(COL_V + cs.start, PROJ_TILE).astype(BF16)

        for rs in row_chunks:
            tasks.append(lambda rs=rs: prenorm(rs))
        for rs in row_chunks:
            tasks.append(lambda rs=rs: forget_gate(rs))
        for col in range(0, SSD_CONV_DIM, PROJ_TILE):
            cs = slice(col, col + PROJ_TILE)
            tasks.append(lambda cs=cs: xbc_tile(cs))
            for rs in row_chunks:
                tasks.append(lambda rs=rs, cs=cs: conv(rs, cs))
        def keep_conv_tail():
            xbc_s[0:CONV_PAD, :] = xbc_s[SUB_BLOCK:SUB_BLOCK + CONV_PAD, :]

        tasks.append(keep_conv_tail)
        ssd_chunks = range(r0 // SSD_CHUNK, (r0 + SUB_BLOCK) // SSD_CHUNK)
        ssd_shared = {ci: {} for ci in ssd_chunks}
        for ci in ssd_chunks:
            tasks.append(lambda ci=ci: ssd_decays(ci, ssd_shared[ci]))
        tasks.append(q_tile)
        tasks.append(k_tile)
        for ci in ssd_chunks:
            for g in range(SSD_GROUPS):
                tasks.append(lambda ci=ci, g=g: ssd_weights(ci, g, ssd_shared[ci]))
        for ci in range(r0 // GLA_CHUNK, (r0 + SUB_BLOCK) // GLA_CHUNK):
            tasks.append(lambda ci=ci: gla_scores(ci))
        for col in range(0, GLA_VAL, PROJ_TILE):
            cs = slice(col, col + PROJ_TILE)
            tasks.append(lambda cs=cs: act_tile(gact_s, COL_G, cs))
        for col in range(0, SSD_INNER, PROJ_TILE):
            cs = slice(col, col + PROJ_TILE)
            tasks.append(lambda cs=cs: act_tile(zact_s, COL_Z, cs))
        for col in range(0, GLA_VAL, PROJ_TILE):
            cs = slice(col, col + PROJ_TILE)
            tasks.append(lambda cs=cs: v_tile(cs))
        return tasks

    def gla_scores(ci):
        rows = slice(ci * GLA_CHUNK, (ci + 1) * GLA_CHUNK)
        for hh in range(GLA_HEADS):
            ks = slice(hh * GLA_DK, (hh + 1) * GLA_DK)
            att = _dot_nt(qd_s[rows, ks], kd_s[rows, ks])
            att_s[hh, rows, :] = jnp.where(causal_g, att, 0.0).astype(BF16)

    def gla_chunk(ci):
        rows = slice(ci * GLA_CHUNK, (ci + 1) * GLA_CHUNK)
        decay = dec_s[ci:ci + 1, :]
        for hh in range(GLA_HEADS):
            ks = slice(hh * GLA_DK, (hh + 1) * GLA_DK)
            vs = slice(hh * GLA_DV, (hh + 1) * GLA_DV)
            v_h = v_s[rows, vs]
            o = _dot(att_s[hh, rows, :], v_h) + _dot_nt(qd_s[rows, ks], gstate_b_s[hh])
            state_t = gstate_s[hh] * decay[:, ks] + _dot_tn(v_h, ke_s[rows, ks])
            gstate_s[hh] = state_t
            gstate_b_s[hh] = state_t.astype(BF16)
            o = (o * lax.rsqrt(jnp.mean(o * o, axis=-1, keepdims=True) + NORM_EPS)) * gnw_ref[...]
            mix_s[rows, vs] = (o * gact_s[rows, vs].astype(F32)).astype(BF16)

    causal_s, tril_s = _tril(SSD_CHUNK, BF16)
    tril_s2 = jnp.concatenate([tril_s, tril_s], axis=1)
    a_neg = -jnp.exp(alog_ref[...])

    def expand_heads(v):
        hi, lo = _split(v)
        return _dot(jnp.concatenate([hi, lo], axis=1), expand_s[...])

    def ssd_decays(ci, shared):
        rows = slice(ci * SSD_CHUNK, (ci + 1) * SSD_CHUNK)
        dt_in = sm_s[rows, :] + dtb_ref[...]
        dt = jnp.maximum(dt_in, 0.0) + _log1p_exp_neg_abs(dt_in)
        dta_hi, dta_lo = _split(dt * a_neg)
        a_cum = _dot(tril_s2, jnp.concatenate([dta_hi, dta_lo], axis=0))
        a_last = a_cum[SSD_CHUNK - 1:SSD_CHUNK, :]
        decay_in_x = expand_heads(jnp.exp(a_cum))
        decin_s[rows, :] = decay_in_x
        cdec_s[ci:ci + 1, :] = decay_in_x[SSD_CHUNK - 1:SSD_CHUNK, :]
        xss_s[rows, :] = (u_s[rows, :] * expand_heads(jnp.exp(a_last - a_cum) * dt)).astype(BF16)
        shared.update(a_cum=a_cum, src_t=(a_cum - jnp.log(dt)).T)

    def ssd_weights(ci, g, shared):
        rows = slice(ci * SSD_CHUNK, (ci + 1) * SSD_CHUNK)
        b_g = bc_s[rows, g * SSD_STATE:(g + 1) * SSD_STATE]
        c_g = bc_s[rows, (SSD_GROUPS + g) * SSD_STATE:(SSD_GROUPS + g + 1) * SSD_STATE]
        a_cum, src_t = shared["a_cum"], shared["src_t"]
        cb = _dot_nt(c_g, b_g)
        for hl in range(SSD_HEADS_PER_GROUP):
            head = g * SSD_HEADS_PER_GROUP + hl
            lane = DT_LANE + head
            seg = jnp.where(causal_s, a_cum[:, lane:lane + 1] - src_t[lane:lane + 1, :], MASKED_LOG)
            w_s[rows, head * SSD_CHUNK:(head + 1) * SSD_CHUNK] = (jnp.exp(seg) * cb).astype(BF16)

    def ssd_group(ci, g):
        rows = slice(ci * SSD_CHUNK, (ci + 1) * SSD_CHUNK)
        gs = slice(g * SSD_GROUP_WIDTH, (g + 1) * SSD_GROUP_WIDTH)
        b_g = bc_s[rows, g * SSD_STATE:(g + 1) * SSD_STATE]
        c_g = bc_s[rows, (SSD_GROUPS + g) * SSD_STATE:(SSD_GROUPS + g + 1) * SSD_STATE]
        y_g = _dot(c_g, sstate_b_s[g]) * decin_s[rows, gs]
        state = sstate_s[g] * cdec_s[ci:ci + 1, gs] + _dot_tn(b_g, xss_s[rows, gs])
        sstate_s[g] = state
        sstate_b_s[g] = state.astype(BF16)
        y_tiles = []
        for pair in range(SSD_HEADS_PER_GROUP // 2):
            ts = slice(gs.start + pair * LANES, gs.start + (pair + 1) * LANES)
            head = g * SSD_HEADS_PER_GROUP + 2 * pair
            x_pair = jnp.concatenate([xlo_s[rows, ts], xhi_s[rows, ts]], axis=0)
            y_tiles.append(_dot(w_s[rows, head * SSD_CHUNK:(head + 2) * SSD_CHUNK], x_pair))
        y_g = y_g + jnp.concatenate(y_tiles, axis=1) + dskip_ref[:, gs] * u_s[rows, gs]
        y_g = y_g * zact_s[rows, gs].astype(F32)
        y_g = (y_g * lax.rsqrt(jnp.mean(y_g * y_g, axis=-1, keepdims=True) + NORM_EPS)) * snw_ref[:, gs]
        mix_s[rows, GLA_VAL + g * SSD_GROUP_WIDTH:GLA_VAL + (g + 1) * SSD_GROUP_WIDTH] = y_g.astype(BF16)

    def out_rows(rs):
        mixed = _dot(mix_s[rs, :], wout_ref[...])
        mixed = (mixed * lax.rsqrt(jnp.mean(mixed * mixed, axis=-1, keepdims=True) + NORM_EPS)) * postw_ref[...]
        out_ref[0, rs, :] = x_ref[0, rs, :] + gate * mixed

    def recur_tasks(r0):
        tasks = []
        gla_per_ssd = SSD_CHUNK // GLA_CHUNK
        for ci in range(r0 // SSD_CHUNK, (r0 + SUB_BLOCK) // SSD_CHUNK):
            for cc, g in zip(range(gla_per_ssd), range(SSD_GROUPS)):
                tasks.append(lambda ci=ci, cc=cc: gla_chunk(ci * gla_per_ssd + cc))
                tasks.append(lambda ci=ci, g=g: ssd_group(ci, g))
        tasks.append(lambda: out_rows(slice(r0, r0 + SUB_BLOCK)))
        return tasks

    def issue_alternating(a, b):
        order = sorted([((i + 0.5) / len(a), 0, i) for i in range(len(a))]
                       + [((i + 0.5) / len(b), 1, i) for i in range(len(b))])
        for _, which, i in order:
            (a, b)[which][i]()

    sub_starts = list(range(0, tb, SUB_BLOCK))
    for task in project_tasks(sub_starts[0]):
        task()
    for prev, cur in zip(sub_starts[:-1], sub_starts[1:]):
        issue_alternating(project_tasks(cur), recur_tasks(prev))
    for task in recur_tasks(sub_starts[-1]):
        task()


def _const_spec(shape):
    zeros = (0,) * len(shape)
    return pl.BlockSpec(shape, lambda b, t: zeros, pipeline_mode=pl.Buffered(1))


def _layer(x, ada, vecs, wmain, wsm, w2, wout):
    bsz, seq, _ = x.shape
    tb = TIME_BLOCK
    consts = (ada, vecs, wmain, wsm, w2, wout)
    return pl.pallas_call(
        _layer_kernel,
        grid=(bsz, seq // tb),
        in_specs=[pl.BlockSpec((1, tb, D_MODEL), lambda b, t: (b, t, 0))]
        + [_const_spec(a.shape) for a in consts],
        out_specs=pl.BlockSpec((1, tb, D_MODEL), lambda b, t: (b, t, 0)),
        out_shape=jax.ShapeDtypeStruct(x.shape, x.dtype),
        scratch_shapes=[
            pltpu.VMEM((tb, D_MODEL), BF16),
            pltpu.VMEM((tb, LANES), F32),
            pltpu.VMEM((SUB_BLOCK + CONV_PAD, SSD_CONV_DIM), F32),
            pltpu.VMEM((tb, SSD_INNER), F32),
            pltpu.VMEM((tb, GLA_KEY), F32),
            pltpu.VMEM((tb, GLA_KEY), BF16),
            pltpu.VMEM((tb, GLA_KEY), BF16),
            pltpu.VMEM((tb, GLA_KEY), BF16),
            pltpu.VMEM((tb // GLA_CHUNK, GLA_KEY), F32),
            pltpu.VMEM((tb, GLA_VAL), BF16),
            pltpu.VMEM((tb, GLA_VAL), BF16),
            pltpu.VMEM((tb, SSD_INNER), BF16),
            pltpu.VMEM((tb, D_MIX), BF16),
            pltpu.VMEM((GLA_HEADS, GLA_DV, GLA_DK), F32),
            pltpu.VMEM((SSD_GROUPS, SSD_STATE, SSD_GROUP_WIDTH), F32),
            pltpu.VMEM((2 * LANES, GLA_KEY), BF16),
            pltpu.VMEM((LANES, GLA_KEY), BF16),
            pltpu.VMEM((2 * LANES, SSD_INNER), BF16),
            pltpu.VMEM((tb, SSD_INNER), BF16),
            pltpu.VMEM((tb, SSD_INNER), BF16),
            pltpu.VMEM((tb, 2 * SSD_GROUPS * SSD_STATE), BF16),
            pltpu.VMEM((GLA_HEADS, GLA_DV, GLA_DK), BF16),
            pltpu.VMEM((SSD_GROUPS, SSD_STATE, SSD_GROUP_WIDTH), BF16),
            pltpu.VMEM((GLA_HEADS, tb, GLA_CHUNK), BF16),
            pltpu.VMEM((tb, SSD_HEADS * SSD_CHUNK), BF16),
            pltpu.VMEM((tb, SSD_INNER), BF16),
            pltpu.VMEM((tb, SSD_INNER), F32),
            pltpu.VMEM((tb // SSD_CHUNK, SSD_INNER), F32),
        ],
        compiler_params=pltpu.CompilerParams(
            dimension_semantics=("arbitrary", "arbitrary"),
            vmem_limit_bytes=VMEM_LIMIT_BYTES),
        name="hybrid_layer",
    )(x, *consts)


def _prep_kernel(wt_ref, gate_ref, dt_ref, wmain_ref, wsm_ref):
    wmain_ref[...] = wt_ref[...].T.astype(BF16)

    @pl.when(pl.program_id(0) == 0)
    def _():
        narrow_t = jnp.concatenate(
            [gate_ref[...], dt_ref[...], jnp.zeros((LANES - DT_LANE - SSD_HEADS, D_MODEL), F32)], axis=0)
        hi, lo = _split(narrow_t.T)
        wsm_ref[:, 0:LANES] = hi
        wsm_ref[:, LANES:2 * LANES] = lo


def _prep_weights(w_t):
    def src_row(j):
        return pl.multiple_of(j * PROJ_TILE + jnp.where(j * PROJ_TILE >= IN_GATE, GLA_GATE_RANK, 0), SUBLANES)

    return pl.pallas_call(
        _prep_kernel,
        grid=(W_MAIN_COLS // PROJ_TILE,),
        in_specs=[pl.BlockSpec((pl.Element(PROJ_TILE), pl.Element(D_MODEL)), lambda j: (src_row(j), 0)),
                  pl.BlockSpec((pl.Element(GLA_GATE_RANK), pl.Element(D_MODEL)), lambda j: (IN_GATE, 0)),
                  pl.BlockSpec((pl.Element(SSD_HEADS), pl.Element(D_MODEL)), lambda j: (IN_DT, 0))],
        out_specs=[pl.BlockSpec((D_MODEL, PROJ_TILE), lambda j: (0, j)),
                   pl.BlockSpec((D_MODEL, 2 * LANES), lambda j: (0, 0))],
        out_shape=[jax.ShapeDtypeStruct((D_MODEL, W_MAIN_COLS), BF16),
                   jax.ShapeDtypeStruct((D_MODEL, 2 * LANES), BF16)],
        compiler_params=pltpu.CompilerParams(dimension_semantics=("arbitrary",)),
        name="prep_weights",
    )(w_t, w_t, w_t)


def _row(v, offset=0):
    tail = VEC_COLS - offset - v.shape[0]
    return [jnp.zeros((offset,), F32), v.astype(F32), jnp.zeros((tail,), F32)]


def kernel(x, c, ada_w, ada_b, pre_norm_w, w_in, gla_gate_w2, gla_gate_b, gla_norm_w, conv_w, conv_b,
           dt_bias, a_log, d_skip, ssd_norm_w, w_out, post_norm_w):
    bsz, seq, _ = x.shape
    assert seq % TIME_BLOCK == 0 and bsz <= SUBLANES
    depth = ada_w.shape[0]
    c_pad = jnp.pad(c, ((0, SUBLANES - bsz), (0, 0)))
    for i in range(depth):
        ada = _ada(c_pad, ada_w[i], ada_b[i][None, :])
        wmain, wsm2 = _prep_weights(w_in[i].T)
        rows = ([_row(pre_norm_w[i]), _row(gla_gate_b[i]), _row(gla_norm_w[i]), _row(conv_b[i])]
                + [_row(conv_w[i, kk]) for kk in range(SSD_CONV)]
                + [_row(dt_bias[i], DT_LANE), _row(a_log[i], DT_LANE),
                   _row(jnp.repeat(d_skip[i], SSD_HEAD_DIM)), _row(ssd_norm_w[i]), _row(post_norm_w[i])]
                + [[jnp.zeros(((VEC_ROWS - VEC_USED) * VEC_COLS,), F32)]])
        vecs = jnp.concatenate([piece for row in rows for piece in row]).reshape(VEC_ROWS, VEC_COLS)
        w2 = jnp.pad(gla_gate_w2[i], ((0, LANES - GLA_GATE_RANK), (0, 0)))
        x = _layer(x, ada, vecs, wmain, wsm2, w2, w_out[i].astype(BF16))
    return x
```

```python
import jax
import jax.numpy as jnp
from jax import lax
from jax.experimental import pallas as pl
from jax.experimental.pallas import tpu as pltpu

F32 = jnp.float32
BF16 = jnp.bfloat16

NORM_EPS = 1e-6
D_MODEL = 1024
GLA_HEADS = 4
GLA_DK = 128
GLA_DV = 256
GLA_KEY = GLA_HEADS * GLA_DK
GLA_VAL = GLA_HEADS * GLA_DV
GLA_GATE_RANK = 16
GLA_TAU = 16.0
GLA_CHUNK = 64
SSD_INNER = 1024
SSD_HEAD_DIM = 64
SSD_HEADS = 16
SSD_GROUPS = 2
SSD_HEADS_PER_GROUP = SSD_HEADS // SSD_GROUPS
SSD_GROUP_WIDTH = SSD_INNER // SSD_GROUPS
SSD_STATE = 128
SSD_CONV = 4
SSD_CHUNK = 128
SSD_CONV_DIM = SSD_INNER + 2 * SSD_GROUPS * SSD_STATE
D_MIX = GLA_VAL + SSD_INNER

LANES = 128
SUBLANES = 8

TIME_BLOCK = 1024
ROW_CHUNK = 128
PROJ_TILE = 512
SUB_BLOCK = 256
COL_QK = 0
COL_V = COL_QK + 2 * GLA_KEY
COL_G = COL_V + GLA_VAL
COL_Z = COL_G + GLA_VAL
COL_XBC = COL_Z + SSD_INNER
W_MAIN_COLS = COL_XBC + SSD_CONV_DIM
DT_LANE = GLA_GATE_RANK
assert DT_LANE + SSD_HEADS <= LANES
IN_GATE = 2 * GLA_KEY + 2 * GLA_VAL
IN_Z = IN_GATE + GLA_GATE_RANK
IN_DT = IN_Z + SSD_INNER + SSD_CONV_DIM
D_PROJ = IN_DT + SSD_HEADS
assert IN_GATE % PROJ_TILE == 0 and W_MAIN_COLS % PROJ_TILE == 0 and SSD_INNER % PROJ_TILE == 0
assert 2 * SSD_HEAD_DIM == LANES
MASKED_LOG = -1e30
(VEC_PRE_W, VEC_GATE_B, VEC_GLA_NORM, VEC_CONV_B, VEC_CONV_W, VEC_DT_BIAS, VEC_A_LOG,
 VEC_D_SKIP, VEC_SSD_NORM, VEC_POST_W) = (0, 1, 2, 3, 4, 4 + SSD_CONV, 5 + SSD_CONV, 6 + SSD_CONV,
                                          7 + SSD_CONV, 8 + SSD_CONV)
VEC_USED = VEC_POST_W + 1
VEC_ROWS = -(-VEC_USED // SUBLANES) * SUBLANES
VEC_COLS = SSD_CONV_DIM
CONV_PAD = SUBLANES
VMEM_LIMIT_BYTES = 62 * 1024 * 1024


def _dot(a, b):
    return jnp.dot(a, b, preferred_element_type=F32)


def _dot_nt(a, b):
    return lax.dot_general(a, b, (((1,), (1,)), ((), ())), preferred_element_type=F32)


def _dot_tn(a, b):
    return lax.dot_general(a, b, (((0,), (0,)), ((), ())), preferred_element_type=F32)


def _split(x):
    hi = x.astype(BF16)
    lo = (x - hi.astype(F32)).astype(BF16)
    return hi, lo


def _silu(x):
    hx = 0.5 * x
    return hx + hx * jnp.tanh(hx)


def _log1p_exp_neg_abs(x):
    return jnp.log(1.0 + jnp.exp(-jnp.abs(x)))


def _tril(n, dtype):
    r = lax.broadcasted_iota(jnp.int32, (n, n), 0)
    c = lax.broadcasted_iota(jnp.int32, (n, n), 1)
    return r >= c, (r >= c).astype(dtype)


def _ada_kernel(c_ref, w_ref, b_ref, o_ref):
    c = c_ref[...]
    ca_hi, ca_lo = _split(_silu(c))
    w_hi, w_lo = _split(w_ref[...])
    o_ref[...] = _dot(ca_hi, w_hi) + _dot(ca_hi, w_lo) + _dot(ca_lo, w_hi) + b_ref[...]


def _ada(c_pad, ada_w, ada_b):
    rows = c_pad.shape[0]
    return pl.pallas_call(
        _ada_kernel,
        grid=(3,),
        in_specs=[
            pl.BlockSpec((rows, D_MODEL), lambda j: (0, 0)),
            pl.BlockSpec((D_MODEL, D_MODEL), lambda j: (0, j)),
            pl.BlockSpec((1, D_MODEL), lambda j: (0, j)),
        ],
        out_specs=pl.BlockSpec((rows, D_MODEL), lambda j: (0, j)),
        out_shape=jax.ShapeDtypeStruct((rows, 3 * D_MODEL), F32),
        compiler_params=pltpu.CompilerParams(dimension_semantics=("arbitrary",)),
        name="adaln",
    )(c_pad, ada_w, ada_b)


def _layer_kernel(x_ref, ada_ref, vec_ref, wmain_ref, wsm_ref, w2_ref, wout_ref,
                  out_ref,
                  h_s, sm_s, xbc_s, u_s, b_s, qd_s, kd_s, ke_s, dec_s, v_s, gact_s, zact_s, mix_s,
                  gstate_s, sstate_s, w2a_s, w2b_s, expand_s, xlo_s, xhi_s, bc_s,
                  gstate_b_s, sstate_b_s, att_s, w_s, xss_s, decin_s, cdec_s):
    tb = TIME_BLOCK
    t = pl.program_id(1)

    @pl.when(t == 0)
    def _():
        gstate_s[...] = jnp.zeros_like(gstate_s)
        sstate_s[...] = jnp.zeros_like(sstate_s)
        gstate_b_s[...] = jnp.zeros_like(gstate_b_s)
        sstate_b_s[...] = jnp.zeros_like(sstate_b_s)
        xbc_s[0:CONV_PAD, :] = jnp.zeros((CONV_PAD, SSD_CONV_DIM), F32)
        w2_hi, w2_lo = _split(w2_ref[...])
        w2a_s[0:LANES, :] = w2_hi
        w2a_s[LANES:2 * LANES, :] = w2_hi
        w2b_s[...] = w2_lo
        head_row = lax.broadcasted_iota(jnp.int32, (2 * LANES, SSD_INNER), 0) % LANES
        head_of_col = lax.broadcasted_iota(jnp.int32, (2 * LANES, SSD_INNER), 1) // SSD_HEAD_DIM + DT_LANE
        expand_s[...] = (head_row == head_of_col).astype(BF16)

    def vec_view(row, width, rows=1):
        return vec_ref.at[row:row + rows, 0:width]

    prew_ref = vec_view(VEC_PRE_W, D_MODEL)
    gb_ref = vec_view(VEC_GATE_B, GLA_KEY)
    gnw_ref = vec_view(VEC_GLA_NORM, GLA_DV)
    cb_ref = vec_view(VEC_CONV_B, SSD_CONV_DIM)
    cw_ref = vec_view(VEC_CONV_W, SSD_CONV_DIM, SSD_CONV)
    dtb_ref = vec_view(VEC_DT_BIAS, LANES)
    alog_ref = vec_view(VEC_A_LOG, LANES)
    dskip_ref = vec_view(VEC_D_SKIP, SSD_INNER)
    snw_ref = vec_view(VEC_SSD_NORM, SSD_INNER)
    postw_ref = vec_view(VEC_POST_W, D_MODEL)

    ada = ada_ref[pl.ds(pl.program_id(0), 1), :]
    shift = ada[:, 0:D_MODEL]
    scale1 = 1.0 + ada[:, D_MODEL:2 * D_MODEL]
    gate = ada[:, 2 * D_MODEL:3 * D_MODEL]
    causal_g, tril_g = _tril(GLA_CHUNK, BF16)
    tril_g2 = jnp.concatenate([tril_g, tril_g], axis=1)

    def project_tasks(r0):
        sub = slice(r0, r0 + SUB_BLOCK)
        row_chunks = [slice(r, r + ROW_CHUNK) for r in range(r0, r0 + SUB_BLOCK, ROW_CHUNK)]
        tasks = []

        def proj(col, width):
            return _dot(h_s[sub, :], wmain_ref[:, col:col + width])

        def prenorm(rs):
            x = x_ref[0, rs, :]
            ms = jnp.mean(x * x, axis=-1, keepdims=True)
            h = (x * lax.rsqrt(ms + NORM_EPS)) * prew_ref[...] * scale1 + shift
            h_hi = h.astype(BF16)
            h_lo = (h - h_hi.astype(F32)).astype(BF16)
            h_s[rs, :] = h_hi
            both = _dot(h_hi, wsm_ref[...])
            sm_s[rs, :] = both[:, 0:LANES] + both[:, LANES:2 * LANES] + _dot(h_lo, wsm_ref[:, 0:LANES])

        def forget_gate(rs):
            sm_hi, sm_lo = _split(sm_s[rs, :])
            sm_both = jnp.concatenate([sm_hi, sm_lo], axis=1)
            gl = _dot(sm_both, w2a_s[...]) + _dot(sm_hi, w2b_s[...]) + gb_ref[...]
            la = (jnp.minimum(gl, 0.0) - _log1p_exp_neg_abs(gl)) * (1.0 / GLA_TAU)
            for cc in range(ROW_CHUNK // GLA_CHUNK):
                la_hi, la_lo = _split(la[cc * GLA_CHUNK:(cc + 1) * GLA_CHUNK, :])
                b = _dot(tril_g2, jnp.concatenate([la_hi, la_lo], axis=0))
                c0 = rs.start + cc * GLA_CHUNK
                b_s[c0:c0 + GLA_CHUNK, :] = b
                ci = c0 // GLA_CHUNK
                dec_s[ci:ci + 1, :] = jnp.exp(b[GLA_CHUNK - 1:GLA_CHUNK, :])

        def xbc_tile(cs):
            xbc_s[CONV_PAD:CONV_PAD + SUB_BLOCK, cs] = proj(COL_XBC + cs.start, PROJ_TILE)

        def conv(rs, cs):
            a = xbc_s[rs.start - r0:rs.start - r0 + ROW_CHUNK + CONV_PAD, cs]
            acc = cw_ref[0:1, cs] * a
            for kk in range(1, SSD_CONV):
                acc = pltpu.roll(acc, 1, 0) + cw_ref[kk:kk + 1, cs] * a
            u = _silu(acc[CONV_PAD:, :] + cb_ref[:, cs])
            if cs.start < SSD_INNER:
                u_s[rs, cs] = u
                first_head = lax.broadcasted_iota(jnp.int32, u.shape, 1) % LANES < SSD_HEAD_DIM
                xlo_s[rs, cs] = jnp.where(first_head, u, 0.0).astype(BF16)
                xhi_s[rs, cs] = jnp.where(first_head, 0.0, u).astype(BF16)
            else:
                bc_s[rs, cs.start - SSD_INNER:cs.stop - SSD_INNER] = u.astype(BF16)

        def q_tile():
            q = proj(COL_QK, GLA_KEY)
            qd_s[sub, :] = (q * (GLA_DK ** -0.5) * jnp.exp(b_s[sub, :])).astype(BF16)

        def k_tile():
            k = proj(COL_QK + GLA_KEY, GLA_KEY)
            for cc in range(SUB_BLOCK // GLA_CHUNK):
                rows = slice(r0 + cc * GLA_CHUNK, r0 + (cc + 1) * GLA_CHUNK)
                b = b_s[rows, :]
                k_c = k[cc * GLA_CHUNK:(cc + 1) * GLA_CHUNK, :]
                kd_s[rows, :] = (k_c * jnp.exp(-b)).astype(BF16)
                ke_s[rows, :] = (k_c * jnp.exp(b[GLA_CHUNK - 1:GLA_CHUNK, :] - b)).astype(BF16)

        def act_tile(dst_s, col, cs):
            dst_s[sub, cs] = _silu(proj(col + cs.start, PROJ_TILE).astype(BF16))

        def v_tile(cs):
            v_s[sub, cs] = proj(COL_V + cs.start, PROJ_TILE).astype(BF16)

        for rs in row_chunks:
            tasks.append(lambda rs=rs: prenorm(rs))
        for rs in row_chunks:
            tasks.append(lambda rs=rs: forget_gate(rs))
        for col in range(0, SSD_CONV_DIM, PROJ_TILE):
            cs = slice(col, col + PROJ_TILE)
            tasks.append(lambda cs=cs: xbc_tile(cs))
            for rs in row_chunks:
                tasks.append(lambda rs=rs, cs=cs: conv(rs, cs))
        def keep_conv_tail():
            xbc_s[0:CONV_PAD, :] = xbc_s[SUB_BLOCK:SUB_BLOCK + CONV_PAD, :]

        tasks.append(keep_conv_tail)
        ssd_chunks = range(r0 // SSD_CHUNK, (r0 + SUB_BLOCK) // SSD_CHUNK)
        ssd_shared = {ci: {} for ci in ssd_chunks}
        for ci in ssd_chunks:
            tasks.append(lambda ci=ci: ssd_decays(ci, ssd_shared[ci]))
        tasks.append(q_tile)
        tasks.append(k_tile)
        vector_tasks = [lambda ci=ci, g=g: ssd_weights(ci, g, ssd_shared[ci])
                        for ci in ssd_chunks for g in range(SSD_GROUPS)]
        vector_tasks += [lambda ci=ci: gla_scores(ci) for ci in range(r0 // GLA_CHUNK, (r0 + SUB_BLOCK) // GLA_CHUNK)]
        tile_tasks = []
        for col in range(0, GLA_VAL, PROJ_TILE):
            cs = slice(col, col + PROJ_TILE)
            tile_tasks.append(lambda cs=cs: act_tile(gact_s, COL_G, cs))
        for col in range(0, SSD_INNER, PROJ_TILE):
            cs = slice(col, col + PROJ_TILE)
            tile_tasks.append(lambda cs=cs: act_tile(zact_s, COL_Z, cs))
        for col in range(0, GLA_VAL, PROJ_TILE):
            cs = slice(col, col + PROJ_TILE)
            tile_tasks.append(lambda cs=cs: v_tile(cs))
        merged = sorted([((i + 0.5) / len(vector_tasks), 0, i) for i in range(len(vector_tasks))]
                        + [((i + 0.5) / len(tile_tasks), 1, i) for i in range(len(tile_tasks))])
        tasks += [(vector_tasks, tile_tasks)[which][i] for _, which, i in merged]
        return tasks

    def gla_scores(ci):
        rows = slice(ci * GLA_CHUNK, (ci + 1) * GLA_CHUNK)
        for hh in range(GLA_HEADS):
            ks = slice(hh * GLA_DK, (hh + 1) * GLA_DK)
            att = _dot_nt(qd_s[rows, ks], kd_s[rows, ks])
            att_s[hh, rows, :] = jnp.where(causal_g, att, 0.0).astype(BF16)

    def gla_chunk(ci):
        rows = slice(ci * GLA_CHUNK, (ci + 1) * GLA_CHUNK)
        decay = dec_s[ci:ci + 1, :]
        for hh in range(GLA_HEADS):
            ks = slice(hh * GLA_DK, (hh + 1) * GLA_DK)
            vs = slice(hh * GLA_DV, (hh + 1) * GLA_DV)
            v_h = v_s[rows, vs]
            o = _dot(att_s[hh, rows, :], v_h) + _dot_nt(qd_s[rows, ks], gstate_b_s[hh])
            state_t = gstate_s[hh] * decay[:, ks] + _dot_tn(v_h, ke_s[rows, ks])
            gstate_s[hh] = state_t
            gstate_b_s[hh] = state_t.astype(BF16)
            o = (o * lax.rsqrt(jnp.mean(o * o, axis=-1, keepdims=True) + NORM_EPS)) * gnw_ref[...]
            mix_s[rows, vs] = (o * gact_s[rows, vs].astype(F32)).astype(BF16)

    causal_s, tril_s = _tril(SSD_CHUNK, BF16)
    tril_s2 = jnp.concatenate([tril_s, tril_s], axis=1)
    a_neg = -jnp.exp(alog_ref[...])

    def expand_heads(v):
        hi, lo = _split(v)
        return _dot(jnp.concatenate([hi, lo], axis=1), expand_s[...])

    def ssd_decays(ci, shared):
        rows = slice(ci * SSD_CHUNK, (ci + 1) * SSD_CHUNK)
        dt_in = sm_s[rows, :] + dtb_ref[...]
        dt = jnp.maximum(dt_in, 0.0) + _log1p_exp_neg_abs(dt_in)
        dta_hi, dta_lo = _split(dt * a_neg)
        a_cum = _dot(tril_s2, jnp.concatenate([dta_hi, dta_lo], axis=0))
        a_last = a_cum[SSD_CHUNK - 1:SSD_CHUNK, :]
        decay_in_x = expand_heads(jnp.exp(a_cum))
        decin_s[rows, :] = decay_in_x
        cdec_s[ci:ci + 1, :] = decay_in_x[SSD_CHUNK - 1:SSD_CHUNK, :]
        xss_s[rows, :] = (u_s[rows, :] * expand_heads(jnp.exp(a_last - a_cum) * dt)).astype(BF16)
        shared.update(a_cum=a_cum, src_t=(a_cum - jnp.log(dt)).T)

    def ssd_weights(ci, g, shared):
        rows = slice(ci * SSD_CHUNK, (ci + 1) * SSD_CHUNK)
        b_g = bc_s[rows, g * SSD_STATE:(g + 1) * SSD_STATE]
        c_g = bc_s[rows, (SSD_GROUPS + g) * SSD_STATE:(SSD_GROUPS + g + 1) * SSD_STATE]
        a_cum, src_t = shared["a_cum"], shared["src_t"]
        cb = _dot_nt(c_g, b_g)
        for hl in range(SSD_HEADS_PER_GROUP):
            head = g * SSD_HEADS_PER_GROUP + hl
            lane = DT_LANE + head
            seg = jnp.where(causal_s, a_cum[:, lane:lane + 1] - src_t[lane:lane + 1, :], MASKED_LOG)
            w_s[rows, head * SSD_CHUNK:(head + 1) * SSD_CHUNK] = (jnp.exp(seg) * cb).astype(BF16)

    def ssd_group(ci, g):
        rows = slice(ci * SSD_CHUNK, (ci + 1) * SSD_CHUNK)
        gs = slice(g * SSD_GROUP_WIDTH, (g + 1) * SSD_GROUP_WIDTH)
        b_g = bc_s[rows, g * SSD_STATE:(g + 1) * SSD_STATE]
        c_g = bc_s[rows, (SSD_GROUPS + g) * SSD_STATE:(SSD_GROUPS + g + 1) * SSD_STATE]
        y_g = _dot(c_g, sstate_b_s[g]) * decin_s[rows, gs]
        state = sstate_s[g] * cdec_s[ci:ci + 1, gs] + _dot_tn(b_g, xss_s[rows, gs])
        sstate_s[g] = state
        sstate_b_s[g] = state.astype(BF16)
        y_tiles = []
        for pair in range(SSD_HEADS_PER_GROUP // 2):
            ts = slice(gs.start + pair * LANES, gs.start + (pair + 1) * LANES)
            head = g * SSD_HEADS_PER_GROUP + 2 * pair
            x_pair = jnp.concatenate([xlo_s[rows, ts], xhi_s[rows, ts]], axis=0)
            y_tiles.append(_dot(w_s[rows, head * SSD_CHUNK:(head + 2) * SSD_CHUNK], x_pair))
        y_g = y_g + jnp.concatenate(y_tiles, axis=1) + dskip_ref[:, gs] * u_s[rows, gs]
        y_g = y_g * zact_s[rows, gs].astype(F32)
        y_g = (y_g * lax.rsqrt(jnp.mean(y_g * y_g, axis=-1, keepdims=True) + NORM_EPS)) * snw_ref[:, gs]
        mix_s[rows, GLA_VAL + g * SSD_GROUP_WIDTH:GLA_VAL + (g + 1) * SSD_GROUP_WIDTH] = y_g.astype(BF16)

    def out_rows(rs):
        mixed = _dot(mix_s[rs, :], wout_ref[...])
        mixed = (mixed * lax.rsqrt(jnp.mean(mixed * mixed, axis=-1, keepdims=True) + NORM_EPS)) * postw_ref[...]
        out_ref[0, rs, :] = x_ref[0, rs, :] + gate * mixed

    def recur_tasks(r0):
        tasks = []
        gla_per_ssd = SSD_CHUNK // GLA_CHUNK
        for ci in range(r0 // SSD_CHUNK, (r0 + SUB_BLOCK) // SSD_CHUNK):
            for cc in range(gla_per_ssd):
                tasks.append(lambda ci=ci, cc=cc: gla_chunk(ci * gla_per_ssd + cc))
            for g in range(SSD_GROUPS):
                tasks.append(lambda ci=ci, g=g: ssd_group(ci, g))
        tasks.append(lambda: out_rows(slice(r0, r0 + SUB_BLOCK)))
        return tasks

    def issue_alternating(a, b):
        order = sorted([((i + 0.5) / len(a), 0, i) for i in range(len(a))]
                       + [((i + 0.5) / len(b), 1, i) for i in range(len(b))])
        for _, which, i in order:
            (a, b)[which][i]()

    sub_starts = list(range(0, tb, SUB_BLOCK))
    for task in project_tasks(sub_starts[0]):
        task()
    for prev, cur in zip(sub_starts[:-1], sub_starts[1:]):
        issue_alternating(project_tasks(cur), recur_tasks(prev))
    for task in recur_tasks(sub_starts[-1]):
        task()


def _const_spec(shape):
    zeros = (0,) * len(shape)
    return pl.BlockSpec(shape, lambda b, t: zeros, pipeline_mode=pl.Buffered(1))


def _layer(x, ada, vecs, wmain, wsm, w2, wout):
    bsz, seq, _ = x.shape
    tb = TIME_BLOCK
    consts = (ada, vecs, wmain, wsm, w2, wout)
    return pl.pallas_call(
        _layer_kernel,
        grid=(bsz, seq // tb),
        in_specs=[pl.BlockSpec((1, tb, D_MODEL), lambda b, t: (b, t, 0))]
        + [_const_spec(a.shape) for a in consts],
        out_specs=pl.BlockSpec((1, tb, D_MODEL), lambda b, t: (b, t, 0)),
        out_shape=jax.ShapeDtypeStruct(x.shape, x.dtype),
        scratch_shapes=[
            pltpu.VMEM((tb, D_MODEL), BF16),
            pltpu.VMEM((tb, LANES), F32),
            pltpu.VMEM((SUB_BLOCK + CONV_PAD, SSD_CONV_DIM), F32),
            pltpu.VMEM((tb, SSD_INNER), F32),
            pltpu.VMEM((tb, GLA_KEY), F32),
            pltpu.VMEM((tb, GLA_KEY), BF16),
            pltpu.VMEM((tb, GLA_KEY), BF16),
            pltpu.VMEM((tb, GLA_KEY), BF16),
            pltpu.VMEM((tb // GLA_CHUNK, GLA_KEY), F32),
            pltpu.VMEM((tb, GLA_VAL), BF16),
            pltpu.VMEM((tb, GLA_VAL), BF16),
            pltpu.VMEM((tb, SSD_INNER), BF16),
            pltpu.VMEM((tb, D_MIX), BF16),
            pltpu.VMEM((GLA_HEADS, GLA_DV, GLA_DK), F32),
            pltpu.VMEM((SSD_GROUPS, SSD_STATE, SSD_GROUP_WIDTH), F32),
            pltpu.VMEM((2 * LANES, GLA_KEY), BF16),
            pltpu.VMEM((LANES, GLA_KEY), BF16),
            pltpu.VMEM((2 * LANES, SSD_INNER), BF16),
            pltpu.VMEM((tb, SSD_INNER), BF16),
            pltpu.VMEM((tb, SSD_INNER), BF16),
            pltpu.VMEM((tb, 2 * SSD_GROUPS * SSD_STATE), BF16),
            pltpu.VMEM((GLA_HEADS, GLA_DV, GLA_DK), BF16),
            pltpu.VMEM((SSD_GROUPS, SSD_STATE, SSD_GROUP_WIDTH), BF16),
            pltpu.VMEM((GLA_HEADS, tb, GLA_CHUNK), BF16),
            pltpu.VMEM((tb, SSD_HEADS * SSD_CHUNK), BF16),
            pltpu.VMEM((tb, SSD_INNER), BF16),
            pltpu.VMEM((tb, SSD_INNER), F32),
            pltpu.VMEM((tb // SSD_CHUNK, SSD_INNER), F32),
        ],
        compiler_params=pltpu.CompilerParams(
            dimension_semantics=("arbitrary", "arbitrary"),
            vmem_limit_bytes=VMEM_LIMIT_BYTES),
        name="hybrid_layer",
    )(x, *consts)


def _prep_kernel(wt_ref, gate_ref, dt_ref, wmain_ref, wsm_ref):
    wmain_ref[...] = wt_ref[...].T.astype(BF16)

    @pl.when(pl.program_id(0) == 0)
    def _():
        narrow_t = jnp.concatenate(
            [gate_ref[...], dt_ref[...], jnp.zeros((LANES - DT_LANE - SSD_HEADS, D_MODEL), F32)], axis=0)
        hi, lo = _split(narrow_t.T)
        wsm_ref[:, 0:LANES] = hi
        wsm_ref[:, LANES:2 * LANES] = lo


def _prep_weights(w_t):
    def src_row(j):
        return pl.multiple_of(j * PROJ_TILE + jnp.where(j * PROJ_TILE >= IN_GATE, GLA_GATE_RANK, 0), SUBLANES)

    return pl.pallas_call(
        _prep_kernel,
        grid=(W_MAIN_COLS // PROJ_TILE,),
        in_specs=[pl.BlockSpec((pl.Element(PROJ_TILE), pl.Element(D_MODEL)), lambda j: (src_row(j), 0)),
                  pl.BlockSpec((pl.Element(GLA_GATE_RANK), pl.Element(D_MODEL)), lambda j: (IN_GATE, 0)),
                  pl.BlockSpec((pl.Element(SSD_HEADS), pl.Element(D_MODEL)), lambda j: (IN_DT, 0))],
        out_specs=[pl.BlockSpec((D_MODEL, PROJ_TILE), lambda j: (0, j)),
                   pl.BlockSpec((D_MODEL, 2 * LANES), lambda j: (0, 0))],
        out_shape=[jax.ShapeDtypeStruct((D_MODEL, W_MAIN_COLS), BF16),
                   jax.ShapeDtypeStruct((D_MODEL, 2 * LANES), BF16)],
        compiler_params=pltpu.CompilerParams(dimension_semantics=("arbitrary",)),
        name="prep_weights",
    )(w_t, w_t, w_t)


def _row(v, offset=0):
    tail = VEC_COLS - offset - v.shape[0]
    return [jnp.zeros((offset,), F32), v.astype(F32), jnp.zeros((tail,), F32)]


def kernel(x, c, ada_w, ada_b, pre_norm_w, w_in, gla_gate_w2, gla_gate_b, gla_norm_w, conv_w, conv_b,
           dt_bias, a_log, d_skip, ssd_norm_w, w_out, post_norm_w):
    bsz, seq, _ = x.shape
    assert seq % TIME_BLOCK == 0 and bsz <= SUBLANES
    depth = ada_w.shape[0]
    c_pad = jnp.pad(c, ((0, SUBLANES - bsz), (0, 0)))
    for i in range(depth):
        ada = _ada(c_pad, ada_w[i], ada_b[i][None, :])
        wmain, wsm2 = _prep_weights(w_in[i].T)
        rows = ([_row(pre_norm_w[i]), _row(gla_gate_b[i]), _row(gla_norm_w[i]), _row(conv_b[i])]
                + [_row(conv_w[i, kk]) for kk in range(SSD_CONV)]
                + [_row(dt_bias[i], DT_LANE), _row(a_log[i], DT_LANE),
                   _row(jnp.repeat(d_skip[i], SSD_HEAD_DIM)), _row(ssd_norm_w[i]), _row(post_norm_w[i])]
                + [[jnp.zeros(((VEC_ROWS - VEC_USED) * VEC_COLS,), F32)]])
        vecs = jnp.concatenate([piece for row in rows for piece in row]).reshape(VEC_ROWS, VEC_COLS)
        w2 = jnp.pad(gla_gate_w2[i], ((0, LANES - GLA_GATE_RANK), (0, 0)))
        x = _layer(x, ada, vecs, wmain, wsm2, w2, w_out[i].astype(BF16))
    return x
```

```python
import jax
import jax.numpy as jnp
from jax import lax
from jax.experimental import pallas as pl
from jax.experimental.pallas import tpu as pltpu

F32 = jnp.float32
BF16 = jnp.bfloat16

NORM_EPS = 1e-6
D_MODEL = 1024
GLA_HEADS = 4
GLA_DK = 128
GLA_DV = 256
GLA_KEY = GLA_HEADS * GLA_DK
GLA_VAL = GLA_HEADS * GLA_DV
GLA_GATE_RANK = 16
GLA_TAU = 16.0
GLA_CHUNK = 64
SSD_INNER = 1024
SSD_HEAD_DIM = 64
SSD_HEADS = 16
SSD_GROUPS = 2
SSD_HEADS_PER_GROUP = SSD_HEADS // SSD_GROUPS
SSD_GROUP_WIDTH = SSD_INNER // SSD_GROUPS
SSD_STATE = 128
SSD_CONV = 4
SSD_CHUNK = 128
SSD_CONV_DIM = SSD_INNER + 2 * SSD_GROUPS * SSD_STATE
D_MIX = GLA_VAL + SSD_INNER

LANES = 128
SUBLANES = 8

TIME_BLOCK = 1024
ROW_CHUNK = 128
PROJ_TILE = 512
SUB_BLOCK = 256
COL_QK = 0
COL_V = COL_QK + 2 * GLA_KEY
COL_G = COL_V + GLA_VAL
COL_Z = COL_G + GLA_VAL
COL_XBC = COL_Z + SSD_INNER
W_MAIN_COLS = COL_XBC + SSD_CONV_DIM
DT_LANE = GLA_GATE_RANK
assert DT_LANE + SSD_HEADS <= LANES
IN_GATE = 2 * GLA_KEY + 2 * GLA_VAL
IN_Z = IN_GATE + GLA_GATE_RANK
IN_DT = IN_Z + SSD_INNER + SSD_CONV_DIM
D_PROJ = IN_DT + SSD_HEADS
assert IN_GATE % PROJ_TILE == 0 and W_MAIN_COLS % PROJ_TILE == 0 and SSD_INNER % PROJ_TILE == 0
assert 2 * SSD_HEAD_DIM == LANES
MASKED_LOG = -1e30
(VEC_PRE_W, VEC_GATE_B, VEC_GLA_NORM, VEC_CONV_B, VEC_CONV_W, VEC_DT_BIAS, VEC_A_LOG,
 VEC_D_SKIP, VEC_SSD_NORM, VEC_POST_W) = (0, 1, 2, 3, 4, 4 + SSD_CONV, 5 + SSD_CONV, 6 + SSD_CONV,
                                          7 + SSD_CONV, 8 + SSD_CONV)
VEC_USED = VEC_POST_W + 1
VEC_ROWS = -(-VEC_USED // SUBLANES) * SUBLANES
VEC_COLS = SSD_CONV_DIM
CONV_PAD = SUBLANES
VMEM_LIMIT_BYTES = 62 * 1024 * 1024


def _dot(a, b):
    return jnp.dot(a, b, preferred_element_type=F32)


def _dot_nt(a, b):
    return lax.dot_general(a, b, (((1,), (1,)), ((), ())), preferred_element_type=F32)


def _dot_tn(a, b):
    return lax.dot_general(a, b, (((0,), (0,)), ((), ())), preferred_element_type=F32)


def _split(x):
    hi = x.astype(BF16)
    lo = (x - hi.astype(F32)).astype(BF16)
    return hi, lo


def _silu(x):
    hx = 0.5 * x
    return hx + hx * jnp.tanh(hx)


def _log1p_exp_neg_abs(x):
    return jnp.log(1.0 + jnp.exp(-jnp.abs(x)))


def _tril(n, dtype):
    r = lax.broadcasted_iota(jnp.int32, (n, n), 0)
    c = lax.broadcasted_iota(jnp.int32, (n, n), 1)
    return r >= c, (r >= c).astype(dtype)


def _ada_kernel(c_ref, w_ref, b_ref, o_ref):
    c = c_ref[...]
    ca_hi, ca_lo = _split(_silu(c))
    w_hi, w_lo = _split(w_ref[...])
    o_ref[...] = _dot(ca_hi, w_hi) + _dot(ca_hi, w_lo) + _dot(ca_lo, w_hi) + b_ref[...]


def _ada(c_pad, ada_w, ada_b):
    rows = c_pad.shape[0]
    return pl.pallas_call(
        _ada_kernel,
        grid=(3,),
        in_specs=[
            pl.BlockSpec((rows, D_MODEL), lambda j: (0, 0)),
            pl.BlockSpec((D_MODEL, D_MODEL), lambda j: (0, j)),
            pl.BlockSpec((1, D_MODEL), lambda j: (0, j)),
        ],
        out_specs=pl.BlockSpec((rows, D_MODEL), lambda j: (0, j)),
        out_shape=jax.ShapeDtypeStruct((rows, 3 * D_MODEL), F32),
        compiler_params=pltpu.CompilerParams(dimension_semantics=("arbitrary",)),
        name="adaln",
    )(c_pad, ada_w, ada_b)


def _layer_kernel(x_ref, ada_ref, vec_ref, wmain_ref, wsm_ref, w2_ref, wout_ref,
                  out_ref,
                  h_s, sm_s, xbc_s, u_s, b_s, qd_s, kd_s, ke_s, dec_s, v_s, gact_s, zact_s, mix_s,
                  gstate_s, sstate_s, w2a_s, w2b_s, expand_s, xlo_s, xhi_s, bc_s,
                  gstate_b_s, sstate_b_s, att_s, w_s, xss_s, decin_s, cdec_s):
    tb = TIME_BLOCK
    t = pl.program_id(1)

    @pl.when(t == 0)
    def _():
        gstate_s[...] = jnp.zeros_like(gstate_s)
        sstate_s[...] = jnp.zeros_like(sstate_s)
        gstate_b_s[...] = jnp.zeros_like(gstate_b_s)
        sstate_b_s[...] = jnp.zeros_like(sstate_b_s)
        xbc_s[0:CONV_PAD, :] = jnp.zeros((CONV_PAD, SSD_CONV_DIM), F32)
        w2_hi, w2_lo = _split(w2_ref[...])
        w2a_s[0:LANES, :] = w2_hi
        w2a_s[LANES:2 * LANES, :] = w2_hi
        w2b_s[...] = w2_lo
        head_row = lax.broadcasted_iota(jnp.int32, (2 * LANES, SSD_INNER), 0) % LANES
        head_of_col = lax.broadcasted_iota(jnp.int32, (2 * LANES, SSD_INNER), 1) // SSD_HEAD_DIM + DT_LANE
        expand_s[...] = (head_row == head_of_col).astype(BF16)

    def vec_view(row, width, rows=1):
        return vec_ref.at[row:row + rows, 0:width]

    prew_ref = vec_view(VEC_PRE_W, D_MODEL)
    gb_ref = vec_view(VEC_GATE_B, GLA_KEY)
    gnw_ref = vec_view(VEC_GLA_NORM, GLA_DV)
    cb_ref = vec_view(VEC_CONV_B, SSD_CONV_DIM)
    cw_ref = vec_view(VEC_CONV_W, SSD_CONV_DIM, SSD_CONV)
    dtb_ref = vec_view(VEC_DT_BIAS, LANES)
    alog_ref = vec_view(VEC_A_LOG, LANES)
    dskip_ref = vec_view(VEC_D_SKIP, SSD_INNER)
    snw_ref = vec_view(VEC_SSD_NORM, SSD_INNER)
    postw_ref = vec_view(VEC_POST_W, D_MODEL)

    ada = ada_ref[pl.ds(pl.program_id(0), 1), :]
    shift = ada[:, 0:D_MODEL]
    scale1 = 1.0 + ada[:, D_MODEL:2 * D_MODEL]
    gate = ada[:, 2 * D_MODEL:3 * D_MODEL]
    causal_g, tril_g = _tril(GLA_CHUNK, BF16)
    tril_g2 = jnp.concatenate([tril_g, tril_g], axis=1)

    def project_tasks(r0):
        sub = slice(r0, r0 + SUB_BLOCK)
        row_chunks = [slice(r, r + ROW_CHUNK) for r in range(r0, r0 + SUB_BLOCK, ROW_CHUNK)]
        tasks = []

        def proj(col, width):
            return _dot(h_s[sub, :], wmain_ref[:, col:col + width])

        def prenorm(rs):
            x = x_ref[0, rs, :]
            ms = jnp.mean(x * x, axis=-1, keepdims=True)
            h = (x * lax.rsqrt(ms + NORM_EPS)) * prew_ref[...] * scale1 + shift
            h_hi = h.astype(BF16)
            h_lo = (h - h_hi.astype(F32)).astype(BF16)
            h_s[rs, :] = h_hi
            both = _dot(h_hi, wsm_ref[...])
            sm_s[rs, :] = both[:, 0:LANES] + both[:, LANES:2 * LANES] + _dot(h_lo, wsm_ref[:, 0:LANES])

        def forget_gate(rs):
            sm_hi, sm_lo = _split(sm_s[rs, :])
            sm_both = jnp.concatenate([sm_hi, sm_lo], axis=1)
            gl = _dot(sm_both, w2a_s[...]) + _dot(sm_hi, w2b_s[...]) + gb_ref[...]
            la = (jnp.minimum(gl, 0.0) - _log1p_exp_neg_abs(gl)) * (1.0 / GLA_TAU)
            for cc in range(ROW_CHUNK // GLA_CHUNK):
                la_hi, la_lo = _split(la[cc * GLA_CHUNK:(cc + 1) * GLA_CHUNK, :])
                b = _dot(tril_g2, jnp.concatenate([la_hi, la_lo], axis=0))
                c0 = rs.start + cc * GLA_CHUNK
                b_s[c0:c0 + GLA_CHUNK, :] = b
                ci = c0 // GLA_CHUNK
                dec_s[ci:ci + 1, :] = jnp.exp(b[GLA_CHUNK - 1:GLA_CHUNK, :])

        def xbc_tile(cs):
            xbc_s[CONV_PAD:CONV_PAD + SUB_BLOCK, cs] = proj(COL_XBC + cs.start, PROJ_TILE)

        def conv(rs, cs):
            a = xbc_s[rs.start - r0:rs.start - r0 + ROW_CHUNK + CONV_PAD, cs]
            acc = cw_ref[0:1, cs] * a
            for kk in range(1, SSD_CONV):
                acc = pltpu.roll(acc, 1, 0) + cw_ref[kk:kk + 1, cs] * a
            u = _silu(acc[CONV_PAD:, :] + cb_ref[:, cs])
            if cs.start < SSD_INNER:
                u_s[rs, cs] = u
                first_head = lax.broadcasted_iota(jnp.int32, u.shape, 1) % LANES < SSD_HEAD_DIM
                xlo_s[rs, cs] = jnp.where(first_head, u, 0.0).astype(BF16)
                xhi_s[rs, cs] = jnp.where(first_head, 0.0, u).astype(BF16)
            else:
                bc_s[rs, cs.start - SSD_INNER:cs.stop - SSD_INNER] = u.astype(BF16)

        def q_tile():
            q = proj(COL_QK, GLA_KEY)
            qd_s[sub, :] = (q * (GLA_DK ** -0.5) * jnp.exp(b_s[sub, :])).astype(BF16)

        def k_tile():
            k = proj(COL_QK + GLA_KEY, GLA_KEY)
            for cc in range(SUB_BLOCK // GLA_CHUNK):
                rows = slice(r0 + cc * GLA_CHUNK, r0 + (cc + 1) * GLA_CHUNK)
                b = b_s[rows, :]
                k_c = k[cc * GLA_CHUNK:(cc + 1) * GLA_CHUNK, :]
                kd_s[rows, :] = (k_c * jnp.exp(-b)).astype(BF16)
                ke_s[rows, :] = (k_c * jnp.exp(b[GLA_CHUNK - 1:GLA_CHUNK, :] - b)).astype(BF16)

        def act_tile(dst_s, col, cs):
            dst_s[sub, cs] = _silu(proj(col + cs.start, PROJ_TILE).astype(BF16))

        def v_tile(cs):
            v_s[sub, cs] = proj(COL_V + cs.start, PROJ_TILE).astype(BF16)

        for rs in row_chunks:
            tasks.append(lambda rs=rs: prenorm(rs))
        for rs in row_chunks:
            tasks.append(lambda rs=rs: forget_gate(rs))
        for col in range(0, SSD_CONV_DIM, PROJ_TILE):
            cs = slice(col, col + PROJ_TILE)
            tasks.append(lambda cs=cs: xbc_tile(cs))
            for rs in row_chunks:
                tasks.append(lambda rs=rs, cs=cs: conv(rs, cs))
        def keep_conv_tail():
            xbc_s[0:CONV_PAD, :] = xbc_s[SUB_BLOCK:SUB_BLOCK + CONV_PAD, :]

        tasks.append(keep_conv_tail)
        ssd_chunks = range(r0 // SSD_CHUNK, (r0 + SUB_BLOCK) // SSD_CHUNK)
        ssd_shared = {ci: {} for ci in ssd_chunks}
        for ci in ssd_chunks:
            tasks.append(lambda ci=ci: ssd_decays(ci, ssd_shared[ci]))
        tasks.append(q_tile)
        tasks.append(k_tile)
        for ci in ssd_chunks:
            for g in range(SSD_GROUPS):
                tasks.append(lambda ci=ci, g=g: ssd_weights(ci, g, ssd_shared[ci]))
        for ci in range(r0 // GLA_CHUNK, (r0 + SUB_BLOCK) // GLA_CHUNK):
            tasks.append(lambda ci=ci: gla_scores(ci))
        for col in range(0, GLA_VAL, PROJ_TILE):
            cs = slice(col, col + PROJ_TILE)
            tasks.append(lambda cs=cs: act_tile(gact_s, COL_G, cs))
        for col in range(0, SSD_INNER, PROJ_TILE):
            cs = slice(col, col + PROJ_TILE)
            tasks.append(lambda cs=cs: act_tile(zact_s, COL_Z, cs))
        for col in range(0, GLA_VAL, PROJ_TILE):
            cs = slice(col, col + PROJ_TILE)
            tasks.append(lambda cs=cs: v_tile(cs))
        return tasks

    def gla_scores(ci):
        rows = slice(ci * GLA_CHUNK, (ci + 1) * GLA_CHUNK)
        for hh in range(GLA_HEADS):
            ks = slice(hh * GLA_DK, (hh + 1) * GLA_DK)
            att = _dot_nt(qd_s[rows, ks], kd_s[rows, ks])
            att_s[hh, rows, :] = jnp.where(causal_g, att, 0.0).astype(BF16)

    def gla_chunk(ci):
        rows = slice(ci * GLA_CHUNK, (ci + 1) * GLA_CHUNK)
        decay = dec_s[ci:ci + 1, :]
        for hh in range(GLA_HEADS):
            ks = slice(hh * GLA_DK, (hh + 1) * GLA_DK)
            vs = slice(hh * GLA_DV, (hh + 1) * GLA_DV)
            v_h = v_s[rows, vs]
            o = _dot(att_s[hh, rows, :], v_h) + _dot_nt(qd_s[rows, ks], gstate_b_s[hh])
            state_t = gstate_s[hh] * decay[:, ks] + _dot_tn(v_h, ke_s[rows, ks])
            gstate_s[hh] = state_t
            gstate_b_s[hh] = state_t.astype(BF16)
            o = (o * lax.rsqrt(jnp.mean(o * o, axis=-1, keepdims=True) + NORM_EPS)) * gnw_ref[...]
            mix_s[rows, vs] = (o * gact_s[rows, vs].astype(F32)).astype(BF16)

    causal_s, tril_s = _tril(SSD_CHUNK, BF16)
    tril_s2 = jnp.concatenate([tril_s, tril_s], axis=1)
    a_neg = -jnp.exp(alog_ref[...])

    def expand_heads(v):
        hi, lo = _split(v)
        return _dot(jnp.concatenate([hi, lo], axis=1), expand_s[...])

    def ssd_decays(ci, shared):
        rows = slice(ci * SSD_CHUNK, (ci + 1) * SSD_CHUNK)
        dt_in = sm_s[rows, :] + dtb_ref[...]
        dt = jnp.maximum(dt_in, 0.0) + _log1p_exp_neg_abs(dt_in)
        dta_hi, dta_lo = _split(dt * a_neg)
        a_cum = _dot(tril_s2, jnp.concatenate([dta_hi, dta_lo], axis=0))
        a_last = a_cum[SSD_CHUNK - 1:SSD_CHUNK, :]
        decay_in_x = expand_heads(jnp.exp(a_cum))
        decin_s[rows, :] = decay_in_x
        cdec_s[ci:ci + 1, :] = decay_in_x[SSD_CHUNK - 1:SSD_CHUNK, :]
        xss_s[rows, :] = (u_s[rows, :] * expand_heads(jnp.exp(a_last - a_cum) * dt)).astype(BF16)
        shared.update(a_cum=a_cum, src_t=(a_cum - jnp.log(dt)).T)

    def ssd_weights(ci, g, shared):
        rows = slice(ci * SSD_CHUNK, (ci + 1) * SSD_CHUNK)
        b_g = bc_s[rows, g * SSD_STATE:(g + 1) * SSD_STATE]
        c_g = bc_s[rows, (SSD_GROUPS + g) * SSD_STATE:(SSD_GROUPS + g + 1) * SSD_STATE]
        a_cum, src_t = shared["a_cum"], shared["src_t"]
        cb = _dot_nt(c_g, b_g)
        for hl in range(SSD_HEADS_PER_GROUP):
            head = g * SSD_HEADS_PER_GROUP + hl
            lane = DT_LANE + head
            seg = jnp.where(causal_s, a_cum[:, lane:lane + 1] - src_t[lane:lane + 1, :], MASKED_LOG)
            w_s[rows, head * SSD_CHUNK:(head + 1) * SSD_CHUNK] = (jnp.exp(seg) * cb).astype(BF16)

    def ssd_group(ci, g):
        rows = slice(ci * SSD_CHUNK, (ci + 1) * SSD_CHUNK)
        gs = slice(g * SSD_GROUP_WIDTH, (g + 1) * SSD_GROUP_WIDTH)
        b_g = bc_s[rows, g * SSD_STATE:(g + 1) * SSD_STATE]
        c_g = bc_s[rows, (SSD_GROUPS + g) * SSD_STATE:(SSD_GROUPS + g + 1) * SSD_STATE]
        y_g = _dot(c_g, sstate_b_s[g]) * decin_s[rows, gs]
        state = sstate_s[g] * cdec_s[ci:ci + 1, gs] + _dot_tn(b_g, xss_s[rows, gs])
        sstate_s[g] = state
        sstate_b_s[g] = state.astype(BF16)
        y_tiles = []
        for pair in range(SSD_HEADS_PER_GROUP // 2):
            ts = slice(gs.start + pair * LANES, gs.start + (pair + 1) * LANES)
            head = g * SSD_HEADS_PER_GROUP + 2 * pair
            x_pair = jnp.concatenate([xlo_s[rows, ts], xhi_s[rows, ts]], axis=0)
            y_tiles.append(_dot(w_s[rows, head * SSD_CHUNK:(head + 2) * SSD_CHUNK], x_pair))
        y_g = y_g + jnp.concatenate(y_tiles, axis=1) + dskip_ref[:, gs] * u_s[rows, gs]
        y_g = y_g * zact_s[rows, gs].astype(F32)
        y_g = (y_g * lax.rsqrt(jnp.mean(y_g * y_g, axis=-1, keepdims=True) + NORM_EPS)) * snw_ref[:, gs]
        mix_s[rows, GLA_VAL + g * SSD_GROUP_WIDTH:GLA_VAL + (g + 1) * SSD_GROUP_WIDTH] = y_g.astype(BF16)

    def out_rows(rs):
        mixed = _dot(mix_s[rs, :], wout_ref[...])
        mixed = (mixed * lax.rsqrt(jnp.mean(mixed * mixed, axis=-1, keepdims=True) + NORM_EPS)) * postw_ref[...]
        out_ref[0, rs, :] = x_ref[0, rs, :] + gate * mixed

    def recur_tasks(r0):
        tasks = []
        gla_per_ssd = SSD_CHUNK // GLA_CHUNK
        for ci in range(r0 // SSD_CHUNK, (r0 + SUB_BLOCK) // SSD_CHUNK):
            for cc in range(gla_per_ssd):
                tasks.append(lambda ci=ci, cc=cc: gla_chunk(ci * gla_per_ssd + cc))
            for g in range(SSD_GROUPS):
                tasks.append(lambda ci=ci, g=g: ssd_group(ci, g))
        rs = slice(r0, r0 + SUB_BLOCK)
        halves = {}
        for half in range(2):
            cs = slice(half * D_MODEL // 2, (half + 1) * D_MODEL // 2)
            tasks.append(lambda half=half, cs=cs: halves.__setitem__(half, _dot(mix_s[rs, :], wout_ref[:, cs])))

        def finish():
            mixed = jnp.concatenate([halves[0], halves[1]], axis=1)
            mixed = (mixed * lax.rsqrt(jnp.mean(mixed * mixed, axis=-1, keepdims=True) + NORM_EPS)) * postw_ref[...]
            out_ref[0, rs, :] = x_ref[0, rs, :] + gate * mixed

        tasks.append(finish)
        return tasks

    def issue_alternating(a, b):
        order = sorted([((i + 0.5) / len(a), 0, i) for i in range(len(a))]
                       + [((i + 0.5) / len(b), 1, i) for i in range(len(b))])
        for _, which, i in order:
            (a, b)[which][i]()

    sub_starts = list(range(0, tb, SUB_BLOCK))
    for task in project_tasks(sub_starts[0]):
        task()
    for prev, cur in zip(sub_starts[:-1], sub_starts[1:]):
        issue_alternating(project_tasks(cur), recur_tasks(prev))
    for task in recur_tasks(sub_starts[-1]):
        task()


def _const_spec(shape):
    zeros = (0,) * len(shape)
    return pl.BlockSpec(shape, lambda b, t: zeros, pipeline_mode=pl.Buffered(1))


def _layer(x, ada, vecs, wmain, wsm, w2, wout):
    bsz, seq, _ = x.shape
    tb = TIME_BLOCK
    consts = (ada, vecs, wmain, wsm, w2, wout)
    return pl.pallas_call(
        _layer_kernel,
        grid=(bsz, seq // tb),
        in_specs=[pl.BlockSpec((1, tb, D_MODEL), lambda b, t: (b, t, 0))]
        + [_const_spec(a.shape) for a in consts],
        out_specs=pl.BlockSpec((1, tb, D_MODEL), lambda b, t: (b, t, 0)),
        out_shape=jax.ShapeDtypeStruct(x.shape, x.dtype),
        scratch_shapes=[
            pltpu.VMEM((tb, D_MODEL), BF16),
            pltpu.VMEM((tb, LANES), F32),
            pltpu.VMEM((SUB_BLOCK + CONV_PAD, SSD_CONV_DIM), F32),
            pltpu.VMEM((tb, SSD_INNER), F32),
            pltpu.VMEM((tb, GLA_KEY), F32),
            pltpu.VMEM((tb, GLA_KEY), BF16),
            pltpu.VMEM((tb, GLA_KEY), BF16),
            pltpu.VMEM((tb, GLA_KEY), BF16),
            pltpu.VMEM((tb // GLA_CHUNK, GLA_KEY), F32),
            pltpu.VMEM((tb, GLA_VAL), BF16),
            pltpu.VMEM((tb, GLA_VAL), BF16),
            pltpu.VMEM((tb, SSD_INNER), BF16),
            pltpu.VMEM((tb, D_MIX), BF16),
            pltpu.VMEM((GLA_HEADS, GLA_DV, GLA_DK), F32),
            pltpu.VMEM((SSD_GROUPS, SSD_STATE, SSD_GROUP_WIDTH), F32),
            pltpu.VMEM((2 * LANES, GLA_KEY), BF16),
            pltpu.VMEM((LANES, GLA_KEY), BF16),
            pltpu.VMEM((2 * LANES, SSD_INNER), BF16),
            pltpu.VMEM((tb, SSD_INNER), BF16),
            pltpu.VMEM((tb, SSD_INNER), BF16),
            pltpu.VMEM((tb, 2 * SSD_GROUPS * SSD_STATE), BF16),
            pltpu.VMEM((GLA_HEADS, GLA_DV, GLA_DK), BF16),
            pltpu.VMEM((SSD_GROUPS, SSD_STATE, SSD_GROUP_WIDTH), BF16),
            pltpu.VMEM((GLA_HEADS, tb, GLA_CHUNK), BF16),
            pltpu.VMEM((tb, SSD_HEADS * SSD_CHUNK), BF16),
            pltpu.VMEM((tb, SSD_INNER), BF16),
            pltpu.VMEM((tb, SSD_INNER), F32),
            pltpu.VMEM((tb // SSD_CHUNK, SSD_INNER), F32),
        ],
        compiler_params=pltpu.CompilerParams(
            dimension_semantics=("arbitrary", "arbitrary"),
            vmem_limit_bytes=VMEM_LIMIT_BYTES),
        name="hybrid_layer",
    )(x, *consts)


def _prep_kernel(wt_ref, gate_ref, dt_ref, wmain_ref, wsm_ref):
    wmain_ref[...] = wt_ref[...].T.astype(BF16)

    @pl.when(pl.program_id(0) == 0)
    def _():
        narrow_t = jnp.concatenate(
            [gate_ref[...], dt_ref[...], jnp.zeros((LANES - DT_LANE - SSD_HEADS, D_MODEL), F32)], axis=0)
        hi, lo = _split(narrow_t.T)
        wsm_ref[:, 0:LANES] = hi
        wsm_ref[:, LANES:2 * LANES] = lo


def _prep_weights(w_t):
    def src_row(j):
        return pl.multiple_of(j * PROJ_TILE + jnp.where(j * PROJ_TILE >= IN_GATE, GLA_GATE_RANK, 0), SUBLANES)

    return pl.pallas_call(
        _prep_kernel,
        grid=(W_MAIN_COLS // PROJ_TILE,),
        in_specs=[pl.BlockSpec((pl.Element(PROJ_TILE), pl.Element(D_MODEL)), lambda j: (src_row(j), 0)),
                  pl.BlockSpec((pl.Element(GLA_GATE_RANK), pl.Element(D_MODEL)), lambda j: (IN_GATE, 0)),
                  pl.BlockSpec((pl.Element(SSD_HEADS), pl.Element(D_MODEL)), lambda j: (IN_DT, 0))],
        out_specs=[pl.BlockSpec((D_MODEL, PROJ_TILE), lambda j: (0, j)),
                   pl.BlockSpec((D_MODEL, 2 * LANES), lambda j: (0, 0))],
        out_shape=[jax.ShapeDtypeStruct((D_MODEL, W_MAIN_COLS), BF16),
                   jax.ShapeDtypeStruct((D_MODEL, 2 * LANES), BF16)],
        compiler_params=pltpu.CompilerParams(dimension_semantics=("arbitrary",)),
        name="prep_weights",
    )(w_t, w_t, w_t)


def _row(v, offset=0):
    tail = VEC_COLS - offset - v.shape[0]
    return [jnp.zeros((offset,), F32), v.astype(F32), jnp.zeros((tail,), F32)]


def kernel(x, c, ada_w, ada_b, pre_norm_w, w_in, gla_gate_w2, gla_gate_b, gla_norm_w, conv_w, conv_b,
           dt_bias, a_log, d_skip, ssd_norm_w, w_out, post_norm_w):
    bsz, seq, _ = x.shape
    assert seq % TIME_BLOCK == 0 and bsz <= SUBLANES
    depth = ada_w.shape[0]
    c_pad = jnp.pad(c, ((0, SUBLANES - bsz), (0, 0)))
    for i in range(depth):
        ada = _ada(c_pad, ada_w[i], ada_b[i][None, :])
        wmain, wsm2 = _prep_weights(w_in[i].T)
        rows = ([_row(pre_norm_w[i]), _row(gla_gate_b[i]), _row(gla_norm_w[i]), _row(conv_b[i])]
                + [_row(conv_w[i, kk]) for kk in range(SSD_CONV)]
                + [_row(dt_bias[i], DT_LANE), _row(a_log[i], DT_LANE),
                   _row(jnp.repeat(d_skip[i], SSD_HEAD_DIM)), _row(ssd_norm_w[i]), _row(post_norm_w[i])]
                + [[jnp.zeros(((VEC_ROWS - VEC_USED) * VEC_COLS,), F32)]])
        vecs = jnp.concatenate([piece for row in rows for piece in row]).reshape(VEC_ROWS, VEC_COLS)
        w2 = jnp.pad(gla_gate_w2[i], ((0, LANES - GLA_GATE_RANK), (0, 0)))
        x = _layer(x, ada, vecs, wmain, wsm2, w2, w_out[i].astype(BF16))
    return x
```
